```python
import math
import jax, jax.numpy as jnp
from jax import lax
import numpy as np

D_MODEL = 1024
BATCH = 4
SEQ = 4096
DEPTH = 2
DEC_BATCH = 128
DEC_SEQ = 8
PAST_LEN = 2048
PAGE_SIZE = 128

N_EVEN = (DEPTH + 1) // 2
N_ODD = DEPTH // 2
H_A = 4
DK_A = D_MODEL // 16
DV_A = D_MODEL // 8
GLA_LOWRANK = 16
GLA_TAU = 16.0
GLA_CHUNK = 64
H_B = 4
DK_B = D_MODEL // 16
DV_B = D_MODEL // 8
DIFF_Q_BLOCK = 128
H_C = 8
DH_C = D_MODEL // 8
MOBA_BLOCK = 256
MOBA_TOPK = 3
MOBA_Q_ROWS = 64
W_A = H_A * DV_A
W_B = H_B * DV_B
W_E = W_A + W_B
W_C = H_C * DH_C
E_SIZES = (H_A * DK_A, H_A * DK_A, W_A, GLA_LOWRANK, H_B * 2 * DK_B, H_B * 2 * DK_B, W_B, W_E)
E_COLS = 2 * H_A * DK_A + W_A + GLA_LOWRANK + 4 * H_B * DK_B + W_B + W_E
O_SIZES = (W_C, W_C, W_C, W_C)
O_COLS = 4 * W_C
EPS = 1e-6
NEG = -1e30

kernel_name = 'hybrid_gla_diff_moba_decoder_step'


def rmsnorm(x, g):
    xf = x.astype(jnp.float32)
    y = xf * lax.rsqrt(jnp.mean(xf * xf, axis=-1, keepdims=True) + EPS)
    return (y * g.astype(jnp.float32)).astype(x.dtype)


def alibi_slopes(n):
    return jnp.array([2.0 ** (-8.0 * (h + 1) / n) for h in range(n)], dtype=jnp.float32)


def _split(z, sizes):
    idx, acc = [], 0
    for s in sizes[:-1]:
        acc += s
        idx.append(acc)
    return jnp.split(z, idx, axis=-1)


def _gather_pages(cache, page_table):
    g = cache[page_table]
    b, n, p = g.shape[:3]
    return g.reshape((b, n * p) + g.shape[3:])


def _modulated_norm(x, c, g, w_ada, b_ada):
    m = jax.nn.silu(c) @ w_ada + b_ada
    shift, scale, gate = jnp.split(m, 3, axis=-1)
    h = rmsnorm(x, g) * (1.0 + scale[:, None]) + shift[:, None]
    return h, gate[:, None]


def gla_chunked(q, k, v, log_a, s0):
    B, T, H, dk = q.shape
    dv = v.shape[-1]
    f32 = jnp.float32
    C = math.gcd(GLA_CHUNK, T)
    n = T // C
    qc = (q.astype(f32) * dk ** -0.5).reshape(B, n, C, H, dk)
    kc = k.astype(f32).reshape(B, n, C, H, dk)
    vc = v.astype(f32).reshape(B, n, C, H, dv)
    cum = jnp.cumsum(log_a.astype(f32).reshape(B, n, C, H, dk), axis=2)
    last = cum[:, :, -1:]
    q_in = qc * jnp.exp(cum)
    k_in = kc * jnp.exp(-cum)
    k_end = kc * jnp.exp(last - cum)
    causal = jnp.tril(jnp.ones((C, C), dtype=bool))
    att = jnp.where(causal, jnp.einsum('bnthk,bnshk->bnhts', q_in, k_in), 0.0)
    o_intra = jnp.einsum('bnhts,bnshv->bnthv', att, vc)

    def step(S, xs):
        qi, ke, vi, dec = xs
        o = jnp.einsum('bthk,bhkv->bthv', qi, S)
        S = S * jnp.exp(dec)[..., None] + jnp.einsum('bthk,bthv->bhkv', ke, vi)
        return S, o

    xs = (jnp.moveaxis(q_in, 1, 0), jnp.moveaxis(k_end, 1, 0), jnp.moveaxis(vc, 1, 0), jnp.moveaxis(last[:, :, 0], 1, 0))
    s_fin, o_inter = lax.scan(step, s0.astype(f32), xs)
    o = o_intra + jnp.moveaxis(o_inter, 0, 1)
    return o.reshape(B, T, H, dv).astype(v.dtype), s_fin


def diff_attention(q, k, v, q_pos, k_pos, slopes, lam):
    B, Tq, H, _ = q.shape
    L = k.shape[1]
    f32 = jnp.float32
    qbs = math.gcd(DIFF_Q_BLOCK, Tq)
    nqb = Tq // qbs
    kf = k.astype(f32).reshape(B, L, H, 2, DK_B)
    vf = v.astype(f32)
    qf = jnp.moveaxis((q.astype(f32) * DK_B ** -0.5).reshape(B, nqb, qbs, H, 2, DK_B), 1, 0)
    pos = q_pos.reshape(nqb, qbs)

    def block(args):
        qb, pb = args
        s = jnp.einsum('bqhmd,bshmd->bhmqs', qb, kf)
        rel = (pb[:, None] - k_pos[None, :]).astype(f32)
        s = s - slopes[:, None, None, None] * rel
        s = jnp.where(rel >= 0, s, NEG)
        p = jax.nn.softmax(s, axis=-1)
        a = p[:, :, 0] - lam * p[:, :, 1]
        return jnp.einsum('bhqs,bshv->bqhv', a, vf)

    o = lax.map(block, (qf, pos))
    return jnp.moveaxis(o, 0, 1).reshape(B, Tq, H, DV_B).astype(v.dtype)


def _q_rows_block(T, B):
    qb = 1
    while 2 * qb * B <= MOBA_Q_ROWS:
        qb *= 2
    return math.gcd(qb, T)


def moba_attention(q, q_pos, k_parts, v_parts, slopes):
    B, Tq, H, dh = q.shape
    f32 = jnp.float32
    BS = MOBA_BLOCK
    L = sum(p.shape[1] for p in k_parts)
    nb = max(-(-L // BS), MOBA_TOPK) + 1
    pad = jnp.zeros((B, nb * BS - L, H, dh), f32)
    kp = jnp.concatenate([p.astype(f32) for p in k_parts] + [pad], axis=1)
    vp = jnp.concatenate([p.astype(f32) for p in v_parts] + [pad], axis=1)
    kb = kp.reshape(B, nb, BS, H, dh)
    vb = vp.reshape(B, nb, BS, H, dh)
    kmean = jnp.mean(kb, axis=2)
    QB = _q_rows_block(Tq, B)
    nqb = Tq // QB
    scale = dh ** -0.5
    qf = jnp.moveaxis(q.astype(f32).reshape(B, nqb, QB, H, dh), 1, 0)
    pos = q_pos.reshape(nqb, QB)
    b_idx = jnp.arange(B)[:, None, None, None]
    h_idx = jnp.arange(H)[None, None, :, None]
    blk = jnp.arange(nb)
    offs = jnp.arange(BS)
    n_sel = MOBA_TOPK * BS

    def block(args):
        qb, pb = args
        own = pb // BS
        g = jnp.einsum('bqhd,bnhd->bqhn', qb, kmean)
        g = jnp.where((blk[None, :] < own[:, None])[None, :, None, :], g, -jnp.inf)
        _, sel = lax.top_k(g, MOBA_TOPK)
        sel_ok = jnp.arange(MOBA_TOPK)[None, :] < own[:, None]
        ks = kb[b_idx, sel, :, h_idx]
        vs = vb[b_idx, sel, :, h_idx]
        s_past = jnp.einsum('bqhd,bqhnsd->bqhns', qb, ks) * scale
        rel_p = (pb[None, :, None, None, None] - (sel[..., None] * BS + offs)).astype(f32)
        s_past = s_past - slopes[None, None, :, None, None] * rel_p
        s_past = jnp.where(sel_ok[None, :, None, :, None], s_past, NEG)
        start = (pb[0] // BS) * BS
        kw = lax.dynamic_slice_in_dim(kp, start, 2 * BS, axis=1)
        vw = lax.dynamic_slice_in_dim(vp, start, 2 * BS, axis=1)
        pos_w = start + jnp.arange(2 * BS)
        rel_w = pb[:, None] - pos_w[None, :]
        s_own = jnp.einsum('bqhd,bshd->bqhs', qb, kw) * scale
        s_own = s_own - slopes[None, None, :, None] * rel_w.astype(f32)[None, :, None, :]
        ok_w = ((pos_w[None, :] // BS) == own[:, None]) & (rel_w >= 0)
        s_own = jnp.where(ok_w[None, :, None, :], s_own, NEG)
        p = jax.nn.softmax(jnp.concatenate([s_past.reshape(B, QB, H, n_sel), s_own], axis=-1), axis=-1)
        o = jnp.einsum('bqhns,bqhnsd->bqhd', p[..., :n_sel].reshape(B, QB, H, MOBA_TOPK, BS), vs)
        return o + jnp.einsum('bqhs,bshd->bqhd', p[..., n_sel:], vw)

    o = lax.map(block, (qf, pos))
    return jnp.moveaxis(o, 0, 1).reshape(B, Tq, H, dh).astype(q.dtype)


def even_mixer(h, q_pos, s0, past_k, past_v, layer_idx, w_in, w_gate, b_gate, g_gla, lq1, lk1, lq2, lk2, g_diff, w_out):
    B, T, _ = h.shape
    z = h @ w_in
    q_a, k_a, v_a, g_low, q_b, k_b, v_b, gate = _split(z, E_SIZES)
    log_a = jax.nn.log_sigmoid((g_low @ w_gate + b_gate).astype(jnp.float32)) / GLA_TAU
    o_a, s_new = gla_chunked(q_a.reshape(B, T, H_A, DK_A), k_a.reshape(B, T, H_A, DK_A),
                             v_a.reshape(B, T, H_A, DV_A), log_a.reshape(B, T, H_A, DK_A), s0)
    o_a = rmsnorm(o_a, g_gla)
    k_b = k_b.reshape(B, T, H_B, 2 * DK_B)
    v_b = v_b.reshape(B, T, H_B, DV_B)
    if past_k is None:
        keys, vals = k_b, v_b
    else:
        keys = jnp.concatenate([past_k, k_b], axis=1)
        vals = jnp.concatenate([past_v, v_b], axis=1)
    k_pos = jnp.arange(keys.shape[1], dtype=jnp.int32)
    lam_init = 0.8 - 0.6 * math.exp(-0.3 * layer_idx)
    lam = (jnp.exp(jnp.sum(lq1 * lk1).astype(jnp.float32)) - jnp.exp(jnp.sum(lq2 * lk2).astype(jnp.float32)) + lam_init)
    o_b = diff_attention(q_b.reshape(B, T, H_B, 2 * DK_B), keys, vals, q_pos, k_pos, alibi_slopes(H_B), lam)
    o_b = rmsnorm(o_b, g_diff) * (1.0 - lam_init)
    o = jnp.concatenate([o_a.reshape(B, T, W_A), o_b.reshape(B, T, W_B)], axis=-1) * jax.nn.silu(gate)
    return o @ w_out, s_new.astype(h.dtype), k_b, v_b


def odd_mixer(h, q_pos, past_k, past_v, w_in, w_out):
    B, T, _ = h.shape
    q, k, v, gate = _split(h @ w_in, O_SIZES)
    q = q.reshape(B, T, H_C, DH_C)
    k = k.reshape(B, T, H_C, DH_C)
    v = v.reshape(B, T, H_C, DH_C)
    if past_k is None:
        k_parts, v_parts = (k,), (v,)
    else:
        k_parts, v_parts = (past_k, k), (past_v, v)
    o = moba_attention(q, q_pos, k_parts, v_parts, alibi_slopes(H_C))
    o = o.reshape(B, T, W_C) * jax.nn.silu(gate)
    return o @ w_out, k, v


def setup_inputs(seed: int = 0) -> dict:
    key = jax.random.key(seed)
    ks = jax.random.split(key, 32)
    f32 = jnp.float32
    n_pages = PAST_LEN // PAGE_SIZE
    n_used = DEC_BATCH * n_pages
    n_pool = n_used + (n_used + 3) // 4

    def nrm(k, shape, scale=1.0):
        return jax.random.normal(k, shape, f32) * scale

    page_table = jax.random.permutation(ks[9], n_pool)[:n_used].reshape(DEC_BATCH, n_pages).astype(jnp.int32)
    return {
        'x_prompt': nrm(ks[0], (BATCH, SEQ, D_MODEL)),
        'x_sample': nrm(ks[1], (DEC_BATCH, DEC_SEQ, D_MODEL)),
        'c_prompt': nrm(ks[2], (BATCH, D_MODEL)),
        'c_sample': nrm(ks[3], (DEC_BATCH, D_MODEL)),
        'state_gla': nrm(ks[4], (N_EVEN, DEC_BATCH, H_A, DK_A, DV_A), 0.5),
        'cache_diff_k': nrm(ks[5], (N_EVEN, n_pool, PAGE_SIZE, H_B, 2 * DK_B)),
        'cache_diff_v': nrm(ks[6], (N_EVEN, n_pool, PAGE_SIZE, H_B, DV_B)),
        'cache_moba_k': nrm(ks[7], (N_ODD, n_pool, PAGE_SIZE, H_C, DH_C)),
        'cache_moba_v': nrm(ks[8], (N_ODD, n_pool, PAGE_SIZE, H_C, DH_C)),
        'page_table': page_table,
        'norm_g': 1.0 + nrm(ks[10], (DEPTH, D_MODEL), 0.02),
        'w_ada': nrm(ks[11], (DEPTH, D_MODEL, 3 * D_MODEL), 0.5 * D_MODEL ** -0.5),
        'b_ada': nrm(ks[12], (DEPTH, 3 * D_MODEL), 0.02),
        'w_in_e': nrm(ks[13], (N_EVEN, D_MODEL, E_COLS), D_MODEL ** -0.5),
        'w_gla_gate': nrm(ks[14], (N_EVEN, GLA_LOWRANK, H_A * DK_A), GLA_LOWRANK ** -0.5),
        'b_gla_gate': nrm(ks[15], (N_EVEN, H_A * DK_A), 0.1),
        'g_gla_norm': 1.0 + nrm(ks[16], (N_EVEN, DV_A), 0.02),
        'lam_q1': nrm(ks[17], (N_EVEN, DK_B), 0.1),
        'lam_k1': nrm(ks[18], (N_EVEN, DK_B), 0.1),
        'lam_q2': nrm(ks[19], (N_EVEN, DK_B), 0.1),
        'lam_k2': nrm(ks[20], (N_EVEN, DK_B), 0.1),
        'g_diff_norm': 1.0 + nrm(ks[21], (N_EVEN, DV_B), 0.02),
        'w_out_e': nrm(ks[22], (N_EVEN, W_E, D_MODEL), W_E ** -0.5),
        'w_in_o': nrm(ks[23], (N_ODD, D_MODEL, O_COLS), D_MODEL ** -0.5),
        'w_out_o': nrm(ks[24], (N_ODD, W_C, D_MODEL), W_C ** -0.5),
        'final_g': 1.0 + nrm(ks[25], (D_MODEL,), 0.02),
    }


def reference(x_prompt, x_sample, c_prompt, c_sample, state_gla, cache_diff_k, cache_diff_v, cache_moba_k, cache_moba_v,
              page_table, norm_g, w_ada, b_ada, w_in_e, w_gla_gate, b_gla_gate, g_gla_norm, lam_q1, lam_k1, lam_q2, lam_k2,
              g_diff_norm, w_out_e, w_in_o, w_out_o, final_g):
    Bp, Tp, _ = x_prompt.shape
    Ts = x_sample.shape[1]
    past_len = page_table.shape[1] * cache_diff_k.shape[2]
    pos_p = jnp.arange(Tp, dtype=jnp.int32)
    pos_s = past_len + jnp.arange(Ts, dtype=jnp.int32)
    yp, ys = x_prompt, x_sample
    gla_p, gla_s, dk_p, dv_p, dk_s, dv_s = [], [], [], [], [], []
    mk_p, mv_p, mk_s, mv_s = [], [], [], []
    for l in range(DEPTH):
        hp, gp = _modulated_norm(yp, c_prompt, norm_g[l], w_ada[l], b_ada[l])
        hs, gs = _modulated_norm(ys, c_sample, norm_g[l], w_ada[l], b_ada[l])
        if l % 2 == 0:
            e = l // 2
            wts = (w_in_e[e], w_gla_gate[e], b_gla_gate[e], g_gla_norm[e], lam_q1[e], lam_k1[e], lam_q2[e], lam_k2[e],
                   g_diff_norm[e], w_out_e[e])
            s0p = jnp.zeros((Bp, H_A, DK_A, DV_A), jnp.float32)
            op, sp, kp_, vp_ = even_mixer(hp, pos_p, s0p, None, None, l, *wts)
            os_, ss, ks_, vs_ = even_mixer(hs, pos_s, state_gla[e], _gather_pages(cache_diff_k[e], page_table),
                                           _gather_pages(cache_diff_v[e], page_table), l, *wts)
            gla_p.append(sp)
            gla_s.append(ss)
            dk_p.append(kp_)
            dv_p.append(vp_)
            dk_s.append(ks_)
            dv_s.append(vs_)
        else:
            o = l // 2
            op, kp_, vp_ = odd_mixer(hp, pos_p, None, None, w_in_o[o], w_out_o[o])
            os_, ks_, vs_ = odd_mixer(hs, pos_s, _gather_pages(cache_moba_k[o], page_table),
                                      _gather_pages(cache_moba_v[o], page_table), w_in_o[o], w_out_o[o])
            mk_p.append(kp_)
            mv_p.append(vp_)
            mk_s.append(ks_)
            mv_s.append(vs_)
        yp = yp + gp * op
        ys = ys + gs * os_
    y_prompt = rmsnorm(yp, final_g)
    y_sample = rmsnorm(ys, final_g)
    gla_state_prompt = jnp.stack(gla_p)
    gla_state_sample = jnp.stack(gla_s)
    diff_k_prompt = jnp.stack(dk_p)
    diff_v_prompt = jnp.stack(dv_p)
    diff_k_sample = jnp.stack(dk_s)
    diff_v_sample = jnp.stack(dv_s)
    moba_k_prompt = jnp.stack(mk_p)
    moba_v_prompt = jnp.stack(mv_p)
    moba_k_sample = jnp.stack(mk_s)
    moba_v_sample = jnp.stack(mv_s)
    return (y_prompt, y_sample, gla_state_prompt, gla_state_sample, diff_k_prompt, diff_v_prompt, diff_k_sample,
            diff_v_sample, moba_k_prompt, moba_v_prompt, moba_k_sample, moba_v_sample)
```

```python
import functools
import math

import jax
import jax.numpy as jnp
from jax import lax
from jax.experimental import pallas as pl
from jax.experimental.pallas import tpu as pltpu

F32 = jnp.float32
BF16 = jnp.bfloat16

D_MODEL = 1024
H_A, DK_A, DV_A = 4, 64, 128
GLA_LOWRANK = 16
GLA_TAU = 16.0
GLA_CHUNK = 64
H_B, DK_B, DV_B = 4, 64, 128
H_C, DH_C = 8, 128
MOBA_BLOCK = 256
MOBA_TOPK = 3
W_A = H_A * DV_A
W_B = H_B * DV_B
W_C = H_C * DH_C
EPS = 1e-6
NEG = -1e30
LANE = 128
VMEM_LIMIT = 56 * 1024 * 1024

HIGHEST = lax.Precision.HIGHEST


def _dot(a, b):
    return jnp.dot(a, b, preferred_element_type=F32)


def _dot_nt(a, b, precision=None):
    return lax.dot_general(a, b, (((1,), (1,)), ((), ())), preferred_element_type=F32, precision=precision)


def _dot_tn(a, b):
    return lax.dot_general(a, b, (((0,), (0,)), ((), ())), preferred_element_type=F32)


def _rms(x, g):
    return x * lax.rsqrt(jnp.mean(x * x, axis=-1, keepdims=True) + EPS) * g


def _silu(x):
    return x * jax.nn.sigmoid(x)


def _params(*sem):
    return pltpu.CompilerParams(dimension_semantics=sem, vmem_limit_bytes=VMEM_LIMIT)


def _ada_kernel(c_ref, w_ref, b_ref, o_ref):
    a = _silu(c_ref[...]).astype(BF16)
    o_ref[0] = _dot(a, w_ref[0].astype(BF16)) + b_ref[0]


def _ada(c_all, w_ada, b_ada):
    depth = w_ada.shape[0]
    mp = c_all.shape[0]
    tn = 1024
    return pl.pallas_call(
        _ada_kernel,
        out_shape=jax.ShapeDtypeStruct((depth, mp, 3 * D_MODEL), F32),
        grid=(depth, 3 * D_MODEL // tn),
        in_specs=[
            pl.BlockSpec((mp, D_MODEL), lambda l, j: (0, 0)),
            pl.BlockSpec((1, D_MODEL, tn), lambda l, j: (l, 0, j)),
            pl.BlockSpec((1, 1, tn), lambda l, j: (l, 0, j)),
        ],
        out_specs=pl.BlockSpec((1, mp, tn), lambda l, j: (l, 0, j)),
        compiler_params=_params("arbitrary", "arbitrary"),
        name="ada_mod",
    )(c_all, w_ada, b_ada.reshape(depth, 1, 3 * D_MODEL))


def _inproj_kernel(x_ref, shift_ref, scale_ref, g_ref, w_ref, *out_refs, splits):
    nb, tt, d = x_ref.shape
    x = x_ref[...]
    h = _rms(x, g_ref[...]) * (1.0 + scale_ref[...]) + shift_ref[...]
    h2 = h.reshape(nb * tt, d).astype(BF16)
    for (start, width, mult), o_ref in zip(splits, out_refs):
        z = _dot(h2, w_ref[:, start:start + width])
        if mult != 1.0:
            z = z * mult
        o_ref[...] = z.reshape(nb, tt, width).astype(o_ref.dtype)


def _inproj(x, shift, scale, g, w, splits, dtypes, nb, tt):
    nseq, t, d = x.shape
    n = w.shape[1]
    grid = (nseq // nb, t // tt)
    xmap = lambda i, j: (i, j, 0)
    out_shape = [jax.ShapeDtypeStruct((nseq, t, wd), dt) for (_, wd, _), dt in zip(splits, dtypes)]
    out_specs = [pl.BlockSpec((nb, tt, wd), xmap) for (_, wd, _) in splits]
    return pl.pallas_call(
        functools.partial(_inproj_kernel, splits=tuple(splits)),
        out_shape=out_shape,
        grid=grid,
        in_specs=[
            pl.BlockSpec((nb, tt, d), xmap),
            pl.BlockSpec((nb, 1, d), lambda i, j: (i, 0, 0)),
            pl.BlockSpec((nb, 1, d), lambda i, j: (i, 0, 0)),
            pl.BlockSpec((1, 1, d), lambda i, j: (0, 0, 0)),
            pl.BlockSpec((d, n), lambda i, j: (0, 0)),
        ],
        out_specs=out_specs,
        compiler_params=_params("arbitrary", "arbitrary"),
        name="norm_inproj",
    )(x, shift, scale, g, w)


def _outproj_kernel(*refs, n_parts, final):
    x_ref = refs[0]
    o_refs = refs[1:1 + n_parts]
    gate_ref, ag_ref, w_ref = refs[1 + n_parts:4 + n_parts]
    rest = refs[4 + n_parts:]
    fg_ref = rest[0] if final else None
    y_ref = rest[-1]
    nb, tt, d = x_ref.shape
    rows = nb * tt
    sg = _silu(gate_ref[...].reshape(rows, -1))
    acc = jnp.zeros((rows, d), F32)
    col = 0
    for o_ref in o_refs:
        wd = o_ref.shape[-1]
        u = (o_ref[...].reshape(rows, wd) * sg[:, col:col + wd]).astype(BF16)
        acc = acc + _dot(u, w_ref[col:col + wd, :])
        col += wd
    y = x_ref[...] + ag_ref[...] * acc.reshape(nb, tt, d)
    if final:
        y = _rms(y, fg_ref[...])
    y_ref[...] = y


def _outproj(x, o_parts, gate, ada_gate, w, final_g, nb, tt):
    nseq, t, d = x.shape
    grid = (nseq // nb, t // tt)
    xmap = lambda i, j: (i, j, 0)
    final = final_g is not None
    in_specs = [pl.BlockSpec((nb, tt, d), xmap)]
    in_specs += [pl.BlockSpec((nb, tt, o.shape[-1]), xmap) for o in o_parts]
    in_specs += [
        pl.BlockSpec((nb, tt, gate.shape[-1]), xmap),
        pl.BlockSpec((nb, 1, d), lambda i, j: (i, 0, 0)),
        pl.BlockSpec(w.shape, lambda i, j: (0, 0)),
    ]
    args = [x, *o_parts, gate, ada_gate, w]
    if final:
        in_specs.append(pl.BlockSpec((1, 1, d), lambda i, j: (0, 0, 0)))
        args.append(final_g)
    return pl.pallas_call(
        functools.partial(_outproj_kernel, n_parts=len(o_parts), final=final),
        out_shape=jax.ShapeDtypeStruct((nseq, t, d), F32),
        grid=grid,
        in_specs=in_specs,
        out_specs=pl.BlockSpec((nb, tt, d), xmap),
        compiler_params=_params("arbitrary", "arbitrary"),
        name="gate_outproj",
    )(*args)


def _gla_rows(q, k, v, glow, wg, bg, gnorm, states, seq_len):
    r = q.shape[0]
    ns = r // seq_len
    la = jax.nn.log_sigmoid(_dot(glow.astype(BF16), wg) + bg) / GLA_TAU
    ri = lax.broadcasted_iota(jnp.int32, (r, r), 0)
    ci = lax.broadcasted_iota(jnp.int32, (r, r), 1)
    same = (ri // seq_len) == (ci // seq_len)
    causal = same & (ci <= ri)
    cum = jnp.dot(causal.astype(F32), la, preferred_element_type=F32, precision=HIGHEST)
    tot = jnp.dot(same.astype(F32), la, preferred_element_type=F32, precision=HIGHEST)
    q_all = q * (DK_A ** -0.5) * jnp.exp(cum)
    k_all = k * jnp.exp(-cum)
    ke_all = k * jnp.exp(tot - cum)
    dec = jnp.exp(tot)
    er = lax.broadcasted_iota(jnp.int32, (DK_A, DK_A), 0)
    ec = lax.broadcasted_iota(jnp.int32, (DK_A, DK_A), 1)
    eye = (er == ec).astype(F32)
    outs = []
    new_states = [[None] * H_A for _ in range(ns)]
    for h in range(H_A):
        ks = slice(h * DK_A, (h + 1) * DK_A)
        vs = slice(h * DV_A, (h + 1) * DV_A)
        q_in = q_all[:, ks]
        k_end = ke_all[:, ks]
        v_h = v[:, vs]
        att = jnp.where(causal, _dot_nt(q_in.astype(BF16), k_all[:, ks].astype(BF16)), 0.0)
        o = _dot(att.astype(BF16), v_h.astype(BF16))
        o_rows = []
        for j in range(ns):
            rs = slice(j * seq_len, (j + 1) * seq_len)
            s_old = states[j][h]
            o_rows.append(o[rs] + _dot(q_in[rs].astype(BF16), s_old.astype(BF16)))
            d_col = jnp.sum(eye * dec[j * seq_len:j * seq_len + 1, ks], axis=1, keepdims=True)
            new_states[j][h] = s_old * d_col + _dot_tn(k_end[rs].astype(BF16), v_h[rs].astype(BF16))
        o = o_rows[0] if ns == 1 else jnp.concatenate(o_rows, axis=0)
        outs.append(_rms(o, gnorm))
    return outs, new_states


def _gla_prompt_kernel(qkv_ref, glow_ref, wg_ref, bg_ref, gn_ref, o_ref, s_ref, st_sc):
    t = pl.program_id(1)
    tt = qkv_ref.shape[1]

    @pl.when(t == 0)
    def _():
        st_sc[...] = jnp.zeros_like(st_sc)

    wg = wg_ref[...]
    bg = bg_ref[...]
    gn = gn_ref[...]
    nq = H_A * DK_A

    def chunk(c, carry):
        r0 = pl.multiple_of(c * GLA_CHUNK, GLA_CHUNK)
        rows = pl.ds(r0, GLA_CHUNK)
        q = qkv_ref[0, rows, 0:nq]
        k = qkv_ref[0, rows, nq:2 * nq]
        v = qkv_ref[0, rows, 2 * nq:2 * nq + W_A]
        states = [[st_sc[h] for h in range(H_A)]]
        outs, new_states = _gla_rows(q, k, v, glow_ref[0, rows, :], wg, bg, gn, states, GLA_CHUNK)
        for h in range(H_A):
            o_ref[0, rows, h * DV_A:(h + 1) * DV_A] = outs[h]
            st_sc[h] = new_states[0][h]
        return carry

    lax.fori_loop(0, tt // GLA_CHUNK, chunk, 0)

    @pl.when(t == pl.num_programs(1) - 1)
    def _():
        s_ref[0] = st_sc[...]


def _gla_prompt(qkv, glow, wg, bg, gn, tt):
    b, t, _ = qkv.shape
    assert t % tt == 0 and tt % GLA_CHUNK == 0
    return pl.pallas_call(
        _gla_prompt_kernel,
        out_shape=[jax.ShapeDtypeStruct((b, t, W_A), F32), jax.ShapeDtypeStruct((b, H_A, DK_A, DV_A), F32)],
        grid=(b, t // tt),
        in_specs=[
            pl.BlockSpec((1, tt, qkv.shape[-1]), lambda i, j: (i, j, 0)),
            pl.BlockSpec((1, tt, LANE), lambda i, j: (i, j, 0)),
            pl.BlockSpec(wg.shape, lambda i, j: (0, 0)),
            pl.BlockSpec(bg.shape, lambda i, j: (0, 0)),
            pl.BlockSpec(gn.shape, lambda i, j: (0, 0)),
        ],
        out_specs=[
            pl.BlockSpec((1, tt, W_A), lambda i, j: (i, j, 0)),
            pl.BlockSpec((1, H_A, DK_A, DV_A), lambda i, j: (i, 0, 0, 0)),
        ],
        scratch_shapes=[pltpu.VMEM((H_A, DK_A, DV_A), F32)],
        compiler_params=_params("arbitrary", "arbitrary"),
        name="gla_prompt",
    )(qkv, glow, wg, bg, gn)


def _gla_sample_kernel(qkv_ref, glow_ref, s0_ref, wg_ref, bg_ref, gn_ref, o_ref, s_ref):
    nb, ts, _ = qkv_ref.shape
    nq = H_A * DK_A
    r = nb * ts
    qkv = qkv_ref[...].reshape(r, qkv_ref.shape[-1])
    glow = glow_ref[...].reshape(r, LANE)
    states = [[s0_ref[j, h] for h in range(H_A)] for j in range(nb)]
    outs, new_states = _gla_rows(qkv[:, 0:nq], qkv[:, nq:2 * nq], qkv[:, 2 * nq:2 * nq + W_A], glow,
                                 wg_ref[...], bg_ref[...], gn_ref[...], states, ts)
    for h in range(H_A):
        o_ref[:, :, h * DV_A:(h + 1) * DV_A] = outs[h].reshape(nb, ts, DV_A)
        for j in range(nb):
            s_ref[j, h] = new_states[j][h]


def _gla_sample(qkv, glow, s0, wg, bg, gn, nb):
    b, ts, _ = qkv.shape
    assert b % nb == 0 and ts % 8 == 0 and ts <= GLA_CHUNK
    return pl.pallas_call(
        _gla_sample_kernel,
        out_shape=[jax.ShapeDtypeStruct((b, ts, W_A), F32), jax.ShapeDtypeStruct((b, H_A, DK_A, DV_A), F32)],
        grid=(b // nb,),
        in_specs=[
            pl.BlockSpec((nb, ts, qkv.shape[-1]), lambda i: (i, 0, 0)),
            pl.BlockSpec((nb, ts, LANE), lambda i: (i, 0, 0)),
            pl.BlockSpec((nb, H_A, DK_A, DV_A), lambda i: (i, 0, 0, 0)),
            pl.BlockSpec(wg.shape, lambda i: (0, 0)),
            pl.BlockSpec(bg.shape, lambda i: (0, 0)),
            pl.BlockSpec(gn.shape, lambda i: (0, 0)),
        ],
        out_specs=[
            pl.BlockSpec((nb, ts, W_A), lambda i: (i, 0, 0)),
            pl.BlockSpec((nb, H_A, DK_A, DV_A), lambda i: (i, 0, 0, 0)),
        ],
        compiler_params=_params("arbitrary"),
        name="gla_sample",
    )(qkv, glow, s0, wg, bg, gn)


def _lambda(l_ref, lam_init):
    lv = l_ref[...]
    s1 = jnp.sum(lv[0:1] * lv[1:2], axis=1, keepdims=True)
    s2 = jnp.sum(lv[2:3] * lv[3:4], axis=1, keepdims=True)
    return jnp.exp(s1) - jnp.exp(s2) + lam_init


def _diff_prompt_kernel(slope_ref, q_ref, k_ref, v_ref, l_ref, gn_ref, o_ref, m_sc, l_sc, acc_sc, *, lam_init):
    h = pl.program_id(1)
    i = pl.program_id(2)
    tq = q_ref.shape[1]
    slope = slope_ref[h]
    q = q_ref[0]
    lane = lax.broadcasted_iota(jnp.int32, q.shape, 1)
    zero = jnp.zeros_like(q)
    qs = jnp.concatenate([jnp.where(lane < DK_B, q, zero), jnp.where(lane >= DK_B, q, zero)], axis=0)
    rr = lax.broadcasted_iota(jnp.int32, (tq, tq), 0)
    cc = lax.broadcasted_iota(jnp.int32, (tq, tq), 1)
    rel0 = (rr - cc).astype(F32)
    rel0 = jnp.concatenate([rel0, rel0], axis=0)
    m_sc[...] = jnp.full_like(m_sc, NEG)
    l_sc[...] = jnp.zeros_like(l_sc)
    acc_sc[...] = jnp.zeros_like(acc_sc)

    def tile(j, masked):
        rows = pl.ds(pl.multiple_of(j * tq, tq), tq)
        kt = k_ref[0, rows, :].astype(BF16)
        vt = v_ref[0, rows, :].astype(BF16)
        s = _dot_nt(qs, kt)
        rel = rel0 + ((i - j) * tq).astype(F32)
        s = s - slope * rel
        if masked:
            s = jnp.where(rel0 >= 0, s, NEG)
        m_old = m_sc[...]
        m_new = jnp.maximum(m_old, jnp.max(s, axis=1, keepdims=True))
        alpha = jnp.exp(m_old - m_new)
        p = jnp.exp(s - m_new)
        l_sc[...] = alpha * l_sc[...] + jnp.sum(p, axis=1, keepdims=True)
        acc_sc[...] = alpha * acc_sc[...] + _dot(p.astype(BF16), vt)
        m_sc[...] = m_new

    def body(j, carry):
        tile(j, False)
        return carry

    lax.fori_loop(0, i, body, 0)
    tile(i, True)
    lam = _lambda(l_ref, lam_init)
    o = acc_sc[...] / l_sc[...]
    o = o[:tq] - lam * o[tq:]
    o_ref[0] = _rms(o, gn_ref[...]) * (1.0 - lam_init)


def _diff_prompt(q, k, v, slopes, lvec, gn, lam_init, tq):
    b, t, _ = q.shape
    assert t % tq == 0
    kv_spec = pl.BlockSpec((1, t, LANE), lambda bi, h, i, s: (bi, 0, h))
    return pl.pallas_call(
        functools.partial(_diff_prompt_kernel, lam_init=lam_init),
        out_shape=jax.ShapeDtypeStruct((b, t, W_B), F32),
        grid_spec=pltpu.PrefetchScalarGridSpec(
            num_scalar_prefetch=1,
            grid=(b, H_B, t // tq),
            in_specs=[
                pl.BlockSpec((1, tq, LANE), lambda bi, h, i, s: (bi, i, h)),
                kv_spec,
                kv_spec,
                pl.BlockSpec(lvec.shape, lambda bi, h, i, s: (0, 0)),
                pl.BlockSpec(gn.shape, lambda bi, h, i, s: (0, 0)),
            ],
            out_specs=pl.BlockSpec((1, tq, LANE), lambda bi, h, i, s: (bi, i, h)),
            scratch_shapes=[
                pltpu.VMEM((2 * tq, 1), F32),
                pltpu.VMEM((2 * tq, 1), F32),
                pltpu.VMEM((2 * tq, DV_B), F32),
            ],
        ),
        compiler_params=_params("arbitrary", "arbitrary", "arbitrary"),
        name="diff_prompt",
    )(slopes, q, k, v, lvec, gn)


def _diff_sample_kernel(pt_ref, q_ref, kn_ref, vn_ref, kp_ref, vp_ref, l_ref, gn_ref, o_ref,
                        wt_sc, s_sc, v_sc, *, lam_init, past_len):
    p = pl.program_id(1)
    npg = pl.num_programs(1)
    ts = q_ref.shape[1]
    page = kp_ref.shape[1]
    width = H_B * 2 * DK_B
    ncol = 2 * LANE
    lpad = s_sc.shape[0]

    @pl.when(p == 0)
    def _():
        q = q_ref[0]
        qt = jnp.broadcast_to(q[None], (ncol // ts, ts, width)).reshape(ncol, width)
        c = lax.broadcasted_iota(jnp.int32, (ncol, width), 0)
        kk = lax.broadcasted_iota(jnp.int32, (ncol, width), 1)
        keep = ((c % LANE) // ts == kk // (2 * DK_B)) & (c // LANE == (kk % (2 * DK_B)) // DK_B)
        wt_sc[...] = jnp.where(keep, qt, 0.0).astype(BF16)
        s_sc[past_len:, :] = jnp.zeros((lpad - past_len, ncol), F32)
        v_sc[past_len:, :] = jnp.zeros((lpad - past_len, v_sc.shape[1]), BF16)

    rows = pl.ds(pl.multiple_of(p * page, page), page)
    s_sc[rows, :] = _dot_nt(kp_ref[0].astype(BF16), wt_sc[...])
    v_sc[rows, :] = vp_ref[0].astype(BF16)

    @pl.when(p == npg - 1)
    def _():
        s_sc[past_len:past_len + ts, :] = _dot_nt(kn_ref[0].astype(BF16), wt_sc[...])
        v_sc[past_len:past_len + ts, :] = vn_ref[0].astype(BF16)
        s = s_sc[...]
        j = lax.broadcasted_iota(jnp.int32, (lpad, ncol), 0)
        c = lax.broadcasted_iota(jnp.int32, (lpad, ncol), 1)
        hh = (c % LANE) // ts
        qi = c % ts
        slope = jnp.exp2(-(8.0 / H_B) * (hh + 1).astype(F32))
        rel = past_len + qi - j
        s = jnp.where((rel >= 0) & (j < past_len + ts), s - slope * rel.astype(F32), NEG)
        m = jnp.max(s, axis=0, keepdims=True)
        e = jnp.exp(s - m)
        pr = e / jnp.sum(e, axis=0, keepdims=True)
        lam = _lambda(l_ref, lam_init)
        a_t = pr[:, :LANE] - lam * pr[:, LANE:]
        o_full = _dot(a_t.T.astype(BF16), v_sc[...])
        gn = gn_ref[...]
        for h in range(H_B):
            o = o_full[h * ts:(h + 1) * ts, h * DV_B:(h + 1) * DV_B]
            o_ref[0, :, h * DV_B:(h + 1) * DV_B] = _rms(o, gn) * (1.0 - lam_init)


def _diff_sample(q, k_new, v_new, cache_k, cache_v, page_table, lvec, gn, lam_init):
    b, ts, width = q.shape
    n_pages = page_table.shape[1]
    page = cache_k.shape[1]
    past_len = n_pages * page
    assert H_B * ts <= LANE and ts % 8 == 0
    lpad = past_len + LANE
    row_spec = pl.BlockSpec((1, ts, width), lambda bi, p, pt: (bi, 0, 0))
    page_spec = pl.BlockSpec((1, page, width), lambda bi, p, pt: (pt[bi * n_pages + p], 0, 0))
    return pl.pallas_call(
        functools.partial(_diff_sample_kernel, lam_init=lam_init, past_len=past_len),
        out_shape=jax.ShapeDtypeStruct((b, ts, W_B), F32),
        grid_spec=pltpu.PrefetchScalarGridSpec(
            num_scalar_prefetch=1,
            grid=(b, n_pages),
            in_specs=[
                row_spec, row_spec, row_spec, page_spec, page_spec,
                pl.BlockSpec(lvec.shape, lambda bi, p, pt: (0, 0)),
                pl.BlockSpec(gn.shape, lambda bi, p, pt: (0, 0)),
            ],
            out_specs=pl.BlockSpec((1, ts, W_B), lambda bi, p, pt: (bi, 0, 0)),
            scratch_shapes=[
                pltpu.VMEM((2 * LANE, width), BF16),
                pltpu.VMEM((lpad, 2 * LANE), F32),
                pltpu.VMEM((lpad, W_B), BF16),
            ],
        ),
        compiler_params=_params("arbitrary", "arbitrary"),
        name="diff_sample",
    )(page_table.reshape(-1), q, k_new, v_new, cache_k, cache_v, lvec, gn)


def _topk_mask(g, valid, n_axis):
    nb = g.shape[n_axis]
    g = jnp.where(valid, g, -jnp.inf)
    idx = lax.broadcasted_iota(jnp.int32, g.shape, n_axis)
    rank = jnp.zeros(g.shape, jnp.int32)
    for m in range(nb):
        gm = lax.slice_in_dim(g, m, m + 1, axis=n_axis)
        beats = (gm > g) | ((gm == g) & (m < idx))
        rank = rank + beats.astype(jnp.int32)
    return (rank < MOBA_TOPK) & valid


def _moba_prompt_kernel(slope_ref, q_ref, k_ref, v_ref, o_ref, km_sc, sel_sc, m_sc, l_sc, acc_sc):
    h = pl.program_id(1)
    i = pl.program_id(2)
    bs = q_ref.shape[1]
    nblk = km_sc.shape[0]
    slope = slope_ref[h]
    scale = DH_C ** -0.5

    @pl.when(i == 0)
    def _():
        def blk(n, carry):
            rows = pl.ds(pl.multiple_of(n * bs, bs), bs)
            km_sc[pl.ds(n, 1), :] = jnp.mean(k_ref[0, rows, :], axis=0, keepdims=True)
            return carry
        lax.fori_loop(0, nblk, blk, 0)

    q = q_ref[0]
    g = _dot_nt(q, km_sc[...], precision=HIGHEST)
    blk_id = lax.broadcasted_iota(jnp.int32, g.shape, 1)
    sel = _topk_mask(g, blk_id < i, 1)
    sel_sc[...] = jnp.where(sel, 0.0, NEG)
    qb = q.astype(BF16)
    rr = lax.broadcasted_iota(jnp.int32, (bs, bs), 0)
    cc = lax.broadcasted_iota(jnp.int32, (bs, bs), 1)
    rel0 = (rr - cc).astype(F32)
    m_sc[...] = jnp.full_like(m_sc, NEG)
    l_sc[...] = jnp.zeros_like(l_sc)
    acc_sc[...] = jnp.zeros_like(acc_sc)

    def tile(j, own):
        rows = pl.ds(pl.multiple_of(j * bs, bs), bs)
        kt = k_ref[0, rows, :].astype(BF16)
        vt = v_ref[0, rows, :].astype(BF16)
        s = _dot_nt(qb, kt) * scale
        rel = rel0 + ((i - j) * bs).astype(F32)
        s = s - slope * rel
        if own:
            s = jnp.where(rel0 >= 0, s, NEG)
        else:
            lane = lax.broadcasted_iota(jnp.int32, sel_sc.shape, 1)
            ok = jnp.sum(jnp.where(lane == j, sel_sc[...], 0.0), axis=1, keepdims=True)
            s = jnp.where(ok < 0.0, NEG, s)
        m_old = m_sc[...]
        m_new = jnp.maximum(m_old, jnp.max(s, axis=1, keepdims=True))
        alpha = jnp.exp(m_old - m_new)
        p = jnp.where(s > 0.5 * NEG, jnp.exp(s - m_new), 0.0)
        l_sc[...] = alpha * l_sc[...] + jnp.sum(p, axis=1, keepdims=True)
        acc_sc[...] = alpha * acc_sc[...] + _dot(p.astype(BF16), vt)
        m_sc[...] = m_new

    def body(j, carry):
        tile(j, False)
        return carry

    lax.fori_loop(0, i, body, 0)
    tile(i, True)
    o_ref[0] = acc_sc[...] / l_sc[...]


def _moba_prompt(q, k, v, slopes):
    b, t, _ = q.shape
    bs = MOBA_BLOCK
    assert t % bs == 0
    nblk = t // bs
    kv_spec = pl.BlockSpec((1, t, LANE), lambda bi, h, i, s: (bi, 0, h))
    q_spec = pl.BlockSpec((1, bs, LANE), lambda bi, h, i, s: (bi, i, h))
    return pl.pallas_call(
        _moba_prompt_kernel,
        out_shape=jax.ShapeDtypeStruct((b, t, W_C), F32),
        grid_spec=pltpu.PrefetchScalarGridSpec(
            num_scalar_prefetch=1,
            grid=(b, H_C, nblk),
            in_specs=[q_spec, kv_spec, kv_spec],
            out_specs=q_spec,
            scratch_shapes=[
                pltpu.VMEM((nblk, DH_C), F32),
                pltpu.VMEM((bs, nblk), F32),
                pltpu.VMEM((bs, 1), F32),
                pltpu.VMEM((bs, 1), F32),
                pltpu.VMEM((bs, DH_C), F32),
            ],
        ),
        compiler_params=_params("arbitrary", "arbitrary", "arbitrary"),
        name="moba_prompt",
    )(slopes, q, k, v)


def _moba_sample_kernel(pt_ref, q_ref, kn_ref, vn_ref, kp_ref, vp_ref, o_ref,
                        wt_sc, wf_sc, ks_sc, s_sc, v_sc, *, past_len):
    p = pl.program_id(1)
    npg = pl.num_programs(1)
    ts = q_ref.shape[1]
    page = kp_ref.shape[1]
    bs = MOBA_BLOCK
    nbp = past_len // bs
    ppb = bs // page
    lpad = s_sc.shape[0]
    scale = DH_C ** -0.5

    @pl.when(p == 0)
    def _():
        q = q_ref[0]
        qt = jnp.broadcast_to(q[None], (LANE // ts, ts, W_C)).reshape(LANE, W_C)
        c = lax.broadcasted_iota(jnp.int32, (LANE, W_C), 0)
        kk = lax.broadcasted_iota(jnp.int32, (LANE, W_C), 1)
        wf = jnp.where(c // ts == kk // DH_C, qt, 0.0)
        wf_sc[...] = wf
        wt_sc[...] = wf.astype(BF16)
        ks_sc[...] = jnp.zeros_like(ks_sc)
        s_sc[past_len:, :] = jnp.zeros((lpad - past_len, LANE), F32)
        v_sc[past_len:, :] = jnp.zeros((lpad - past_len, W_C), BF16)

    rows = pl.ds(pl.multiple_of(p * page, page), page)
    kp = kp_ref[0]
    s_sc[rows, :] = _dot_nt(kp.astype(BF16), wt_sc[...])
    v_sc[rows, :] = vp_ref[0].astype(BF16)
    blk = p // ppb
    ks_sc[pl.ds(blk, 1), :] = ks_sc[pl.ds(blk, 1), :] + jnp.sum(kp, axis=0, keepdims=True)

    @pl.when(p == npg - 1)
    def _():
        s_sc[past_len:past_len + ts, :] = _dot_nt(kn_ref[0].astype(BF16), wt_sc[...])
        v_sc[past_len:past_len + ts, :] = vn_ref[0].astype(BF16)
        kmean = ks_sc[...] * (1.0 / bs)
        g = _dot_nt(kmean, wf_sc[...], precision=HIGHEST)
        sel = _topk_mask(g, jnp.full(g.shape, True), 0)
        selb = jnp.where(sel, 0.0, NEG)
        s = s_sc[...] * scale
        j = lax.broadcasted_iota(jnp.int32, (lpad, LANE), 0)
        c = lax.broadcasted_iota(jnp.int32, (lpad, LANE), 1)
        hh = c // ts
        qi = c % ts
        slope = jnp.exp2(-(8.0 / H_C) * (hh + 1).astype(F32))
        rel = past_len + qi - j
        s = s - slope * rel.astype(F32)
        own_ok = (rel >= 0) & (j >= past_len)
        parts = []
        for n in range(nbp):
            parts.append(jnp.where(selb[n:n + 1, :] < 0.0, NEG, s[n * bs:(n + 1) * bs]))
        parts.append(jnp.where(own_ok[past_len:], s[past_len:], NEG))
        s = jnp.concatenate(parts, axis=0)
        m = jnp.max(s, axis=0, keepdims=True)
        e = jnp.where(s > 0.5 * NEG, jnp.exp(s - m), 0.0)
        pr = e / jnp.sum(e, axis=0, keepdims=True)
        o_full = _dot(pr.T.astype(BF16), v_sc[...])
        for h in range(H_C):
            o_ref[0, :, h * DH_C:(h + 1) * DH_C] = o_full[h * ts:(h + 1) * ts, h * DH_C:(h + 1) * DH_C]


def _moba_sample(q, k_new, v_new, cache_k, cache_v, page_table):
    b, ts, width = q.shape
    n_pages = page_table.shape[1]
    page = cache_k.shape[1]
    past_len = n_pages * page
    bs = MOBA_BLOCK
    assert past_len % bs == 0 and bs % page == 0 and ts <= bs and H_C * ts <= LANE and ts % 8 == 0
    assert past_len // bs >= MOBA_TOPK
    lpad = past_len + LANE
    row_spec = pl.BlockSpec((1, ts, width), lambda bi, p, pt: (bi, 0, 0))
    page_spec = pl.BlockSpec((1, page, width), lambda bi, p, pt: (pt[bi * n_pages + p], 0, 0))
    return pl.pallas_call(
        functools.partial(_moba_sample_kernel, past_len=past_len),
        out_shape=jax.ShapeDtypeStruct((b, ts, W_C), F32),
        grid_spec=pltpu.PrefetchScalarGridSpec(
            num_scalar_prefetch=1,
            grid=(b, n_pages),
            in_specs=[row_spec, row_spec, row_spec, page_spec, page_spec],
            out_specs=row_spec,
            scratch_shapes=[
                pltpu.VMEM((LANE, W_C), BF16),
                pltpu.VMEM((LANE, W_C), F32),
                pltpu.VMEM((past_len // bs, W_C), F32),
                pltpu.VMEM((lpad, LANE), F32),
                pltpu.VMEM((lpad, W_C), BF16),
            ],
        ),
        compiler_params=_params("arbitrary", "arbitrary"),
        name="moba_sample",
    )(page_table.reshape(-1), q, k_new, v_new, cache_k, cache_v)


def _alibi_slopes(n):
    return jnp.array([2.0 ** (-8.0 * (h + 1) / n) for h in range(n)], dtype=F32)


def kernel(x_prompt, x_sample, c_prompt, c_sample, state_gla, cache_diff_k, cache_diff_v, cache_moba_k, cache_moba_v,
           page_table, norm_g, w_ada, b_ada, w_in_e, w_gla_gate, b_gla_gate, g_gla_norm, lam_q1, lam_k1, lam_q2,
           lam_k2, g_diff_norm, w_out_e, w_in_o, w_out_o, final_g):
    bp, tp, d = x_prompt.shape
    bs_, ts, _ = x_sample.shape
    depth = norm_g.shape[0]
    n_pool, page = cache_diff_k.shape[1], cache_diff_k.shape[2]
    assert d == D_MODEL

    tt_p = min(512, tp)
    nb_s = min(64, bs_)
    tq_diff = min(256, tp)

    mrows = bp + bs_
    mpad = -(-mrows // 8) * 8
    c_all = jnp.concatenate([c_prompt, c_sample, jnp.zeros((mpad - mrows, d), F32)], axis=0)
    mod = _ada(c_all, w_ada, b_ada)

    def mod_parts(l, lo, hi):
        m = mod[l, lo:hi]
        return m[:, None, 0:d], m[:, None, d:2 * d], m[:, None, 2 * d:3 * d]

    slopes_b = _alibi_slopes(H_B)
    slopes_c = _alibi_slopes(H_C)
    nqa = H_A * DK_A
    yp, ys = x_prompt, x_sample
    gla_p, gla_s, dk_p, dv_p, dk_s, dv_s, mk_p, mv_p, mk_s, mv_s = ([] for _ in range(10))
    for l in range(depth):
        shift_p, scale_p, gate_p = mod_parts(l, 0, bp)
        shift_s, scale_s, gate_s = mod_parts(l, bp, bp + bs_)
        g_l = norm_g[l].reshape(1, 1, d)
        last = l == depth - 1
        fg = final_g.reshape(1, 1, d) if last else None
        if l % 2 == 0:
            e = l // 2
            w = w_in_e[e]
            c0 = 2 * nqa + W_A
            c1 = c0 + GLA_LOWRANK
            w_cat = jnp.concatenate([w[:, :c0], w[:, c0:c1], jnp.zeros((d, LANE - GLA_LOWRANK), F32), w[:, c1:]],
                                    axis=1).astype(BF16)
            o0 = c0 + LANE
            splits = [(0, c0, 1.0), (c0, LANE, 1.0), (o0, 2 * H_B * DK_B, DK_B ** -0.5),
                      (o0 + 512, 512, 1.0), (o0 + 1024, 512, 1.0), (o0 + 1536, W_A + W_B, 1.0)]
            dtypes = [F32, F32, BF16, F32, F32, F32]
            wg = jnp.concatenate([w_gla_gate[e], jnp.zeros((LANE - GLA_LOWRANK, nqa), F32)], axis=0).astype(BF16)
            bg = b_gla_gate[e].reshape(1, nqa)
            gn_a = g_gla_norm[e].reshape(1, DV_A)
            gn_b = g_diff_norm[e].reshape(1, DV_B)
            lvec = jnp.stack([lam_q1[e], lam_k1[e], lam_q2[e], lam_k2[e]])
            lam_init = 0.8 - 0.6 * math.exp(-0.3 * l)
            w_out = w_out_e[e].astype(BF16)

            qkv_a, glow, q_b, k_b, v_b, gate = _inproj(yp, shift_p, scale_p, g_l, w_cat, splits, dtypes, 1, tt_p)
            o_a, s_new = _gla_prompt(qkv_a, glow, wg, bg, gn_a, tt_p)
            o_b = _diff_prompt(q_b, k_b, v_b, slopes_b, lvec, gn_b, lam_init, tq_diff)
            yp = _outproj(yp, [o_a, o_b], gate, gate_p, w_out, fg, 1, tt_p)
            gla_p.append(s_new)
            dk_p.append(k_b.reshape(bp, tp, H_B, 2 * DK_B))
            dv_p.append(v_b.reshape(bp, tp, H_B, DV_B))

            dtypes_s = [F32] * len(dtypes)
            qkv_a, glow, q_b, k_b, v_b, gate = _inproj(ys, shift_s, scale_s, g_l, w_cat, splits, dtypes_s, nb_s, ts)
            o_a, s_new = _gla_sample(qkv_a, glow, state_gla[e], wg, bg, gn_a, min(8, bs_))
            ck = cache_diff_k[e].reshape(n_pool, page, H_B * 2 * DK_B)
            cv = cache_diff_v[e].reshape(n_pool, page, W_B)
            o_b = _diff_sample(q_b, k_b, v_b, ck, cv, page_table, lvec, gn_b, lam_init)
            ys = _outproj(ys, [o_a, o_b], gate, gate_s, w_out, fg, nb_s, ts)
            gla_s.append(s_new)
            dk_s.append(k_b.reshape(bs_, ts, H_B, 2 * DK_B))
            dv_s.append(v_b.reshape(bs_, ts, H_B, DV_B))
        else:
            o = l // 2
            w_cat = w_in_o[o].astype(BF16)
            splits = [(0, W_C, 1.0), (W_C, W_C, 1.0), (2 * W_C, W_C, 1.0), (3 * W_C, W_C, 1.0)]
            dtypes = [F32, F32, F32, F32]
            w_out = w_out_o[o].astype(BF16)

            q, k, v, gate = _inproj(yp, shift_p, scale_p, g_l, w_cat, splits, dtypes, 1, tt_p)
            att = _moba_prompt(q, k, v, slopes_c)
            yp = _outproj(yp, [att], gate, gate_p, w_out, fg, 1, tt_p)
            mk_p.append(k.reshape(bp, tp, H_C, DH_C))
            mv_p.append(v.reshape(bp, tp, H_C, DH_C))

            q, k, v, gate = _inproj(ys, shift_s, scale_s, g_l, w_cat, splits, dtypes, nb_s, ts)
            ck = cache_moba_k[o].reshape(n_pool, page, W_C)
            cv = cache_moba_v[o].reshape(n_pool, page, W_C)
            att = _moba_sample(q, k, v, ck, cv, page_table)
            ys = _outproj(ys, [att], gate, gate_s, w_out, fg, nb_s, ts)
            mk_s.append(k.reshape(bs_, ts, H_C, DH_C))
            mv_s.append(v.reshape(bs_, ts, H_C, DH_C))
    return (yp, ys, jnp.stack(gla_p), jnp.stack(gla_s), jnp.stack(dk_p), jnp.stack(dv_p), jnp.stack(dk_s),
            jnp.stack(dv_s), jnp.stack(mk_p), jnp.stack(mv_p), jnp.stack(mk_s), jnp.stack(mv_s))
```

```python
import functools
import math

import jax
import jax.numpy as jnp
import numpy as np
from jax import lax
from jax.experimental import pallas as pl
from jax.experimental.pallas import tpu as pltpu

F32 = jnp.float32
BF16 = jnp.bfloat16

D_MODEL = 1024
H_A, DK_A, DV_A = 4, 64, 128
GLA_LOWRANK = 16
GLA_TAU = 16.0
GLA_CHUNK = 64
H_B, DK_B, DV_B = 4, 64, 128
H_C, DH_C = 8, 128
MOBA_BLOCK = 256
MOBA_TOPK = 3
W_A = H_A * DV_A
W_B = H_B * DV_B
W_C = H_C * DH_C
EPS = 1e-6
NEG = -1e30
LANE = 128
VT_ROWS = LANE + 16
KV_GROUP = 4
VMEM_LIMIT = 56 * 1024 * 1024

HIGHEST = lax.Precision.HIGHEST


def _dot(a, b):
    return jnp.dot(a, b, preferred_element_type=F32)


def _dot_nt(a, b, precision=None):
    return lax.dot_general(a, b, (((1,), (1,)), ((), ())), preferred_element_type=F32, precision=precision)


def _dot_tn(a, b):
    return lax.dot_general(a, b, (((0,), (0,)), ((), ())), preferred_element_type=F32)


def _rms(x, g):
    return x * lax.rsqrt(jnp.mean(x * x, axis=-1, keepdims=True) + EPS) * g


def _silu(x):
    return x * jax.nn.sigmoid(x)


def _params(*sem):
    return pltpu.CompilerParams(dimension_semantics=sem, vmem_limit_bytes=VMEM_LIMIT)


def _ada_kernel(c_ref, w_ref, b_ref, o_ref):
    a = _silu(c_ref[...]).astype(BF16)
    o_ref[0] = _dot(a, w_ref[0].astype(BF16)) + b_ref[0]


def _ada(c_all, w_ada, b_ada):
    depth = w_ada.shape[0]
    mp = c_all.shape[0]
    tn = 1024
    return pl.pallas_call(
        _ada_kernel,
        out_shape=jax.ShapeDtypeStruct((depth, mp, 3 * D_MODEL), F32),
        grid=(depth, 3 * D_MODEL // tn),
        in_specs=[
            pl.BlockSpec((mp, D_MODEL), lambda l, j: (0, 0)),
            pl.BlockSpec((1, D_MODEL, tn), lambda l, j: (l, 0, j)),
            pl.BlockSpec((1, 1, tn), lambda l, j: (l, 0, j)),
        ],
        out_specs=pl.BlockSpec((1, mp, tn), lambda l, j: (l, 0, j)),
        compiler_params=_params("arbitrary", "arbitrary"),
        name="ada_mod",
    )(c_all, w_ada, b_ada.reshape(depth, 1, 3 * D_MODEL))


def _inproj_kernel(x_ref, shift_ref, scale_ref, g_ref, w_ref, *out_refs, splits):
    nb, tt, d = x_ref.shape
    x = x_ref[...]
    h = _rms(x, g_ref[...]) * (1.0 + scale_ref[...]) + shift_ref[...]
    h2 = h.reshape(nb * tt, d).astype(BF16)
    for (start, width, mult), o_ref in zip(splits, out_refs):
        z = _dot(h2, w_ref[:, start:start + width])
        if mult != 1.0:
            z = z * mult
        o_ref[...] = z.reshape(nb, tt, width).astype(o_ref.dtype)


def _inproj(x, shift, scale, g, w, splits, dtypes, nb, tt):
    nseq, t, d = x.shape
    n = w.shape[1]
    grid = (nseq // nb, t // tt)
    xmap = lambda i, j: (i, j, 0)
    out_shape = [jax.ShapeDtypeStruct((nseq, t, wd), dt) for (_, wd, _), dt in zip(splits, dtypes)]
    out_specs = [pl.BlockSpec((nb, tt, wd), xmap) for (_, wd, _) in splits]
    return pl.pallas_call(
        functools.partial(_inproj_kernel, splits=tuple(splits)),
        out_shape=out_shape,
        grid=grid,
        in_specs=[
            pl.BlockSpec((nb, tt, d), xmap),
            pl.BlockSpec((nb, 1, d), lambda i, j: (i, 0, 0)),
            pl.BlockSpec((nb, 1, d), lambda i, j: (i, 0, 0)),
            pl.BlockSpec((1, 1, d), lambda i, j: (0, 0, 0)),
            pl.BlockSpec((d, n), lambda i, j: (0, 0)),
        ],
        out_specs=out_specs,
        compiler_params=_params("arbitrary", "arbitrary"),
        name="norm_inproj",
    )(x, shift, scale, g, w)


def _outproj_kernel(*refs, n_parts, final):
    x_ref = refs[0]
    o_refs = refs[1:1 + n_parts]
    gate_ref, ag_ref, w_ref = refs[1 + n_parts:4 + n_parts]
    rest = refs[4 + n_parts:]
    fg_ref = rest[0] if final else None
    y_ref = rest[-1]
    nb, tt, d = x_ref.shape
    rows = nb * tt
    sg = _silu(gate_ref[...].reshape(rows, -1))
    acc = jnp.zeros((rows, d), F32)
    col = 0
    for o_ref in o_refs:
        wd = o_ref.shape[-1]
        u = (o_ref[...].reshape(rows, wd) * sg[:, col:col + wd]).astype(BF16)
        acc = acc + _dot(u, w_ref[col:col + wd, :])
        col += wd
    y = x_ref[...] + ag_ref[...] * acc.reshape(nb, tt, d)
    if final:
        y = _rms(y, fg_ref[...])
    y_ref[...] = y


def _outproj(x, o_parts, gate, ada_gate, w, final_g, nb, tt):
    nseq, t, d = x.shape
    grid = (nseq // nb, t // tt)
    xmap = lambda i, j: (i, j, 0)
    final = final_g is not None
    in_specs = [pl.BlockSpec((nb, tt, d), xmap)]
    in_specs += [pl.BlockSpec((nb, tt, o.shape[-1]), xmap) for o in o_parts]
    in_specs += [
        pl.BlockSpec((nb, tt, gate.shape[-1]), xmap),
        pl.BlockSpec((nb, 1, d), lambda i, j: (i, 0, 0)),
        pl.BlockSpec(w.shape, lambda i, j: (0, 0)),
    ]
    args = [x, *o_parts, gate, ada_gate, w]
    if final:
        in_specs.append(pl.BlockSpec((1, 1, d), lambda i, j: (0, 0, 0)))
        args.append(final_g)
    return pl.pallas_call(
        functools.partial(_outproj_kernel, n_parts=len(o_parts), final=final),
        out_shape=jax.ShapeDtypeStruct((nseq, t, d), F32),
        grid=grid,
        in_specs=in_specs,
        out_specs=pl.BlockSpec((nb, tt, d), xmap),
        compiler_params=_params("arbitrary", "arbitrary"),
        name="gate_outproj",
    )(*args)


def _gla_rows(q, k, v, glow, wg, bg, gnorm, states, seq_len):
    r = q.shape[0]
    ns = r // seq_len
    la = jax.nn.log_sigmoid(_dot(glow.astype(BF16), wg) + bg) / GLA_TAU
    ri = lax.broadcasted_iota(jnp.int32, (r, r), 0)
    ci = lax.broadcasted_iota(jnp.int32, (r, r), 1)
    same = (ri // seq_len) == (ci // seq_len)
    causal = same & (ci <= ri)
    cum = jnp.dot(causal.astype(F32), la, preferred_element_type=F32, precision=HIGHEST)
    tot = jnp.dot(same.astype(F32), la, preferred_element_type=F32, precision=HIGHEST)
    q_all = q * (DK_A ** -0.5) * jnp.exp(cum)
    k_all = k * jnp.exp(-cum)
    ke_all = k * jnp.exp(tot - cum)
    dec = jnp.exp(tot)
    er = lax.broadcasted_iota(jnp.int32, (DK_A, DK_A), 0)
    ec = lax.broadcasted_iota(jnp.int32, (DK_A, DK_A), 1)
    eye = (er == ec).astype(F32)
    outs = []
    new_states = [[None] * H_A for _ in range(ns)]
    for h in range(H_A):
        ks = slice(h * DK_A, (h + 1) * DK_A)
        vs = slice(h * DV_A, (h + 1) * DV_A)
        q_in = q_all[:, ks]
        k_end = ke_all[:, ks]
        v_h = v[:, vs]
        att = jnp.where(causal, _dot_nt(q_in.astype(BF16), k_all[:, ks].astype(BF16)), 0.0)
        o = _dot(att.astype(BF16), v_h.astype(BF16))
        o_rows = []
        for j in range(ns):
            rs = slice(j * seq_len, (j + 1) * seq_len)
            s_old = states[j][h]
            o_rows.append(o[rs] + _dot(q_in[rs].astype(BF16), s_old.astype(BF16)))
            d_col = jnp.sum(eye * dec[j * seq_len:j * seq_len + 1, ks], axis=1, keepdims=True)
            new_states[j][h] = s_old * d_col + _dot_tn(k_end[rs].astype(BF16), v_h[rs].astype(BF16))
        o = o_rows[0] if ns == 1 else jnp.concatenate(o_rows, axis=0)
        outs.append(_rms(o, gnorm))
    return outs, new_states


def _gla_prompt_kernel(qkv_ref, glow_ref, wg_ref, bg_ref, gn_ref, o_ref, s_ref, st_sc):
    t = pl.program_id(1)
    tt = qkv_ref.shape[1]

    @pl.when(t == 0)
    def _():
        st_sc[...] = jnp.zeros_like(st_sc)

    wg = wg_ref[...]
    bg = bg_ref[...]
    gn = gn_ref[...]
    nq = H_A * DK_A

    def chunk(c, carry):
        r0 = pl.multiple_of(c * GLA_CHUNK, GLA_CHUNK)
        rows = pl.ds(r0, GLA_CHUNK)
        q = qkv_ref[0, rows, 0:nq]
        k = qkv_ref[0, rows, nq:2 * nq]
        v = qkv_ref[0, rows, 2 * nq:2 * nq + W_A]
        states = [[st_sc[h] for h in range(H_A)]]
        outs, new_states = _gla_rows(q, k, v, glow_ref[0, rows, :], wg, bg, gn, states, GLA_CHUNK)
        for h in range(H_A):
            o_ref[0, rows, h * DV_A:(h + 1) * DV_A] = outs[h]
            st_sc[h] = new_states[0][h]
        return carry

    lax.fori_loop(0, tt // GLA_CHUNK, chunk, 0)

    @pl.when(t == pl.num_programs(1) - 1)
    def _():
        s_ref[0] = st_sc[...]


def _gla_prompt(qkv, glow, wg, bg, gn, tt):
    b, t, _ = qkv.shape
    assert t % tt == 0 and tt % GLA_CHUNK == 0
    return pl.pallas_call(
        _gla_prompt_kernel,
        out_shape=[jax.ShapeDtypeStruct((b, t, W_A), F32), jax.ShapeDtypeStruct((b, H_A, DK_A, DV_A), F32)],
        grid=(b, t // tt),
        in_specs=[
            pl.BlockSpec((1, tt, qkv.shape[-1]), lambda i, j: (i, j, 0)),
            pl.BlockSpec((1, tt, LANE), lambda i, j: (i, j, 0)),
            pl.BlockSpec(wg.shape, lambda i, j: (0, 0)),
            pl.BlockSpec(bg.shape, lambda i, j: (0, 0)),
            pl.BlockSpec(gn.shape, lambda i, j: (0, 0)),
        ],
        out_specs=[
            pl.BlockSpec((1, tt, W_A), lambda i, j: (i, j, 0)),
            pl.BlockSpec((1, H_A, DK_A, DV_A), lambda i, j: (i, 0, 0, 0)),
        ],
        scratch_shapes=[pltpu.VMEM((H_A, DK_A, DV_A), F32)],
        compiler_params=_params("arbitrary", "arbitrary"),
        name="gla_prompt",
    )(qkv, glow, wg, bg, gn)


def _gla_sample_kernel(qkv_ref, glow_ref, s0_ref, wg_ref, bg_ref, gn_ref, o_ref, s_ref):
    nb, ts, _ = qkv_ref.shape
    nq = H_A * DK_A
    r = nb * ts
    qkv = qkv_ref[...].reshape(r, qkv_ref.shape[-1])
    glow = glow_ref[...].reshape(r, LANE)
    states = [[s0_ref[j, h] for h in range(H_A)] for j in range(nb)]
    outs, new_states = _gla_rows(qkv[:, 0:nq], qkv[:, nq:2 * nq], qkv[:, 2 * nq:2 * nq + W_A], glow,
                                 wg_ref[...], bg_ref[...], gn_ref[...], states, ts)
    for h in range(H_A):
        o_ref[:, :, h * DV_A:(h + 1) * DV_A] = outs[h].reshape(nb, ts, DV_A)
        for j in range(nb):
            s_ref[j, h] = new_states[j][h]


def _gla_sample(qkv, glow, s0, wg, bg, gn, nb):
    b, ts, _ = qkv.shape
    assert b % nb == 0 and ts % 8 == 0 and ts <= GLA_CHUNK
    return pl.pallas_call(
        _gla_sample_kernel,
        out_shape=[jax.ShapeDtypeStruct((b, ts, W_A), F32), jax.ShapeDtypeStruct((b, H_A, DK_A, DV_A), F32)],
        grid=(b // nb,),
        in_specs=[
            pl.BlockSpec((nb, ts, qkv.shape[-1]), lambda i: (i, 0, 0)),
            pl.BlockSpec((nb, ts, LANE), lambda i: (i, 0, 0)),
            pl.BlockSpec((nb, H_A, DK_A, DV_A), lambda i: (i, 0, 0, 0)),
            pl.BlockSpec(wg.shape, lambda i: (0, 0)),
            pl.BlockSpec(bg.shape, lambda i: (0, 0)),
            pl.BlockSpec(gn.shape, lambda i: (0, 0)),
        ],
        out_specs=[
            pl.BlockSpec((nb, ts, W_A), lambda i: (i, 0, 0)),
            pl.BlockSpec((nb, H_A, DK_A, DV_A), lambda i: (i, 0, 0, 0)),
        ],
        compiler_params=_params("arbitrary"),
        name="gla_sample",
    )(qkv, glow, s0, wg, bg, gn)


def _attend_cols(kaug_sc, vt_sc, q_aug, i, tq, acc_sc):
    ncol = q_aug.shape[0]
    t = kaug_sc.shape[0]
    gk = min(KV_GROUP * tq, t)
    kr = lax.broadcasted_iota(jnp.int32, (gk, ncol), 0)
    qc = lax.broadcasted_iota(jnp.int32, (gk, ncol), 1) % tq
    for v in range(t // gk):
        head = v * gk

        @pl.when((i * tq) // gk == v)
        def _(head=head):
            s_t = _dot_nt(kaug_sc[head:head + gk, :], q_aug)
            s_t = jnp.where(kr + head <= qc + i * tq, s_t, NEG)
            m = jnp.max(s_t, axis=0, keepdims=True)
            if head > 0:
                s_h = _dot_nt(kaug_sc[0:head, :], q_aug)
                m = jnp.maximum(m, jnp.max(s_h, axis=0, keepdims=True))
            acc = _dot(vt_sc[:, head:head + gk], jnp.exp(s_t - m).astype(BF16))
            if head > 0:
                acc = acc + _dot(vt_sc[:, 0:head], jnp.exp(s_h - m).astype(BF16))
            acc_sc[...] = acc


def _fill_kv(k_ref, v_ref, kaug_sc, vt_sc, tk, aug_fn, km_sc=None):
    nblk = kaug_sc.shape[0] // tk
    dv = v_ref.shape[-1]
    rowv = lax.broadcasted_iota(jnp.int32, (VT_ROWS - dv, tk), 0)
    ones_rows = jnp.where(rowv == 0, 1.0, 0.0).astype(BF16)
    for n in range(nblk):
        rows = slice(n * tk, (n + 1) * tk)
        kt = k_ref[0, rows, :]
        if km_sc is not None:
            km_sc[n:n + 1, :] = jnp.mean(kt, axis=0, keepdims=True)
        kaug_sc[rows, 0:LANE] = kt.astype(BF16)
        kaug_sc[rows, LANE:] = aug_fn(n).astype(BF16)
        vt_sc[0:dv, rows] = v_ref[0, rows, :].T.astype(BF16)
        vt_sc[dv:, rows] = ones_rows


def _lambda(l_ref, lam_init):
    lv = l_ref[...]
    s1 = jnp.sum(lv[0:1] * lv[1:2], axis=1, keepdims=True)
    s2 = jnp.sum(lv[2:3] * lv[3:4], axis=1, keepdims=True)
    return jnp.exp(s1) - jnp.exp(s2) + lam_init


def _diff_prompt_kernel(slope_ref, q_ref, k_ref, v_ref, l_ref, gn_ref, o_ref, kaug_sc, vt_sc, acc_sc, *, lam_init):
    h = pl.program_id(1)
    i = pl.program_id(2)
    tq = q_ref.shape[1]

    @pl.when(i == 0)
    def _():
        slope = slope_ref[h]
        c = lax.broadcasted_iota(jnp.int32, (tq, LANE), 0).astype(F32)
        col = lax.broadcasted_iota(jnp.int32, (tq, LANE), 1)

        def aug(n):
            return jnp.where(col == 0, slope * float(tq * n), jnp.where(col == 1, slope * c, 0.0))

        _fill_kv(k_ref, v_ref, kaug_sc, vt_sc, tq, aug)

    q = q_ref[0]
    lane = lax.broadcasted_iota(jnp.int32, q.shape, 1)
    zero = jnp.zeros_like(q)
    ones2 = jnp.where(lane < 2, 1.0, 0.0).astype(BF16)
    q_aug = jnp.concatenate([
        jnp.concatenate([jnp.where(lane < DK_B, q, zero), ones2], axis=1),
        jnp.concatenate([jnp.where(lane >= DK_B, q, zero), ones2], axis=1)], axis=0)
    _attend_cols(kaug_sc, vt_sc, q_aug, i, tq, acc_sc)
    acc = acc_sc[...]
    o_t = acc[0:DV_B] / acc[DV_B:DV_B + 1]
    lam = _lambda(l_ref, lam_init)
    o = (o_t[:, :tq] - lam * o_t[:, tq:]).T
    o_ref[0] = _rms(o, gn_ref[...]) * (1.0 - lam_init)


def _diff_prompt(q, k, v, slopes, lvec, gn, lam_init, tq):
    b, t, _ = q.shape
    assert t % min(KV_GROUP * tq, t) == 0
    nblk = t // tq
    kv_spec = pl.BlockSpec((1, t, LANE), lambda bi, h, i, s: (bi, 0, h))
    return pl.pallas_call(
        functools.partial(_diff_prompt_kernel, lam_init=lam_init),
        out_shape=jax.ShapeDtypeStruct((b, t, W_B), F32),
        grid_spec=pltpu.PrefetchScalarGridSpec(
            num_scalar_prefetch=1,
            grid=(b, H_B, nblk),
            in_specs=[
                pl.BlockSpec((1, tq, LANE), lambda bi, h, i, s: (bi, i, h)),
                kv_spec,
                kv_spec,
                pl.BlockSpec(lvec.shape, lambda bi, h, i, s: (0, 0)),
                pl.BlockSpec(gn.shape, lambda bi, h, i, s: (0, 0)),
            ],
            out_specs=pl.BlockSpec((1, tq, LANE), lambda bi, h, i, s: (bi, i, h)),
            scratch_shapes=[
                pltpu.VMEM((t, 2 * LANE), BF16),
                pltpu.VMEM((VT_ROWS, t), BF16),
                pltpu.VMEM((VT_ROWS, 2 * tq), F32),
            ],
        ),
        compiler_params=_params("arbitrary", "arbitrary", "arbitrary"),
        name="diff_prompt",
    )(slopes, q, k, v, lvec, gn)


def _topk_mask(g, valid, n_axis):
    nb = g.shape[n_axis]
    g = jnp.where(valid, g, -jnp.inf)
    idx = lax.broadcasted_iota(jnp.int32, g.shape, n_axis)
    rank = jnp.zeros(g.shape, jnp.int32)
    for m in range(nb):
        gm = lax.slice_in_dim(g, m, m + 1, axis=n_axis)
        beats = (gm > g) | ((gm == g) & (m < idx))
        rank = rank + beats.astype(jnp.int32)
    return (rank < MOBA_TOPK) & valid


def _moba_prompt_kernel(slope_ref, q_ref, k_ref, v_ref, o_ref, km_sc, kaug_sc, vt_sc, acc_sc):
    h = pl.program_id(1)
    i = pl.program_id(2)
    bs = q_ref.shape[1]
    nblk = km_sc.shape[0]

    @pl.when(i == 0)
    def _():
        slope = slope_ref[h]
        c = lax.broadcasted_iota(jnp.int32, (bs, LANE), 0).astype(F32)
        col = lax.broadcasted_iota(jnp.int32, (bs, LANE), 1)

        def aug(n):
            return jnp.where(col == n, 1.0, jnp.where(col == nblk, slope * float(bs * n),
                                                      jnp.where(col == nblk + 1, slope * c, 0.0)))

        _fill_kv(k_ref, v_ref, kaug_sc, vt_sc, bs, aug, km_sc)

    q = q_ref[0]
    g = _dot_nt(km_sc[...], q, precision=HIGHEST)
    blk = lax.broadcasted_iota(jnp.int32, g.shape, 0)
    sel = _topk_mask(g, blk < i, 0)
    selb = jnp.where(sel | (blk == i), 0.0, NEG)
    row = lax.broadcasted_iota(jnp.int32, (LANE, bs), 0)
    coef = jnp.concatenate([selb, jnp.zeros((LANE - nblk, bs), F32)], axis=0)
    coef = jnp.where((row == nblk) | (row == nblk + 1), 1.0, coef)
    q_aug = jnp.concatenate([q * (DH_C ** -0.5), coef.T], axis=1).astype(BF16)
    _attend_cols(kaug_sc, vt_sc, q_aug, i, bs, acc_sc)
    acc = acc_sc[...]
    o_ref[0] = (acc[0:DH_C] / acc[DH_C:DH_C + 1]).T


def _moba_prompt(q, k, v, slopes):
    b, t, _ = q.shape
    bs = MOBA_BLOCK
    assert t % min(KV_GROUP * bs, t) == 0
    nblk = t // bs
    assert nblk + 2 <= LANE
    kv_spec = pl.BlockSpec((1, t, LANE), lambda bi, h, i, s: (bi, 0, h))
    q_spec = pl.BlockSpec((1, bs, LANE), lambda bi, h, i, s: (bi, i, h))
    return pl.pallas_call(
        _moba_prompt_kernel,
        out_shape=jax.ShapeDtypeStruct((b, t, W_C), F32),
        grid_spec=pltpu.PrefetchScalarGridSpec(
            num_scalar_prefetch=1,
            grid=(b, H_C, nblk),
            in_specs=[q_spec, kv_spec, kv_spec],
            out_specs=q_spec,
            scratch_shapes=[
                pltpu.VMEM((nblk, DH_C), F32),
                pltpu.VMEM((t, 2 * LANE), BF16),
                pltpu.VMEM((VT_ROWS, t), BF16),
                pltpu.VMEM((VT_ROWS, bs), F32),
            ],
        ),
        compiler_params=_params("arbitrary", "arbitrary", "arbitrary"),
        name="moba_prompt",
    )(slopes, q, k, v)


def _score_cols_mask(n_col, n_heads, n_maps, dk, ts):
    c = np.arange(n_col)[:, None]
    f = np.arange(n_heads * n_maps * dk)[None, :]
    return (((c % LANE) // ts == f // (n_maps * dk)) & (c // LANE == (f % (n_maps * dk)) // dk)).astype(np.float32)


def _score_bias(n_col, n_heads, ts, past_len, lpad):
    j = np.arange(lpad)[:, None]
    c = np.arange(n_col)[None, :]
    hh = (c % LANE) // ts
    rel = past_len + c % ts - j
    slope = np.array([2.0 ** (-8.0 * (h + 1) / n_heads) for h in range(LANE // ts + 1)], np.float32)[hh]
    bias = np.where(hh < n_heads, -slope * rel.astype(np.float32), np.float32(0.0))
    return np.where((rel >= 0) & (j < past_len + ts), bias, np.float32(NEG)).astype(np.float32)


def _page_scores(q, wmask_ref, kp_refs, kn_ref, s_sc, past_len):
    ts, width = q.shape
    ncol = s_sc.shape[1]
    page = kp_refs[0].shape[1]
    wt = (jnp.broadcast_to(q[None], (ncol // ts, ts, width)).reshape(ncol, width) * wmask_ref[...]).astype(BF16)
    for g, kp_ref in enumerate(kp_refs):
        s_sc[g * page:(g + 1) * page, :] = _dot_nt(kp_ref[0].astype(BF16), wt)
    s_sc[past_len:past_len + ts, :] = _dot_nt(kn_ref[0].astype(BF16), wt)
    s_sc[past_len + ts:, :] = jnp.zeros((s_sc.shape[0] - past_len - ts, ncol), F32)


def _page_values(a_t, vp_refs, vn_ref, past_len):
    page = vp_refs[0].shape[1]
    ts, width = vn_ref.shape[1], vn_ref.shape[2]
    o = jnp.zeros((LANE, width), F32)
    for g, vp_ref in enumerate(vp_refs):
        o = o + _dot(a_t[g * page:(g + 1) * page].T.astype(BF16), vp_ref[0].astype(BF16))
    v_tail = jnp.concatenate([vn_ref[0], jnp.zeros((LANE - ts, width), F32)], axis=0)
    return o + _dot(a_t[past_len:].T.astype(BF16), v_tail.astype(BF16))


def _softmax_rows(s):
    e = jnp.exp(s - jnp.max(s, axis=0, keepdims=True))
    return e * (1.0 / jnp.sum(e, axis=0, keepdims=True))


def _diff_sample_kernel(pt_ref, q_ref, kn_ref, vn_ref, wmask_ref, bias_ref, l_ref, gn_ref, *rest,
                        n_pages, lam_init, past_len):
    kp_refs, vp_refs = rest[:n_pages], rest[n_pages:2 * n_pages]
    o_ref, s_sc = rest[2 * n_pages:]
    ts = q_ref.shape[1]
    _page_scores(q_ref[0], wmask_ref, kp_refs, kn_ref, s_sc, past_len)
    pr = _softmax_rows(s_sc[...] + bias_ref[...])
    lam = _lambda(l_ref, lam_init)
    a_t = pr[:, :LANE] - lam * pr[:, LANE:]
    o_full = _page_values(a_t, vp_refs, vn_ref, past_len)
    gn = gn_ref[...]
    for h in range(H_B):
        o = o_full[h * ts:(h + 1) * ts, h * DV_B:(h + 1) * DV_B]
        o_ref[0, :, h * DV_B:(h + 1) * DV_B] = _rms(o, gn) * (1.0 - lam_init)


def _moba_sample_kernel(pt_ref, q_ref, kn_ref, vn_ref, wmask_ref, bias_ref, *rest, n_pages, past_len):
    kp_refs, vp_refs = rest[:n_pages], rest[n_pages:2 * n_pages]
    o_ref, s_sc = rest[2 * n_pages:]
    ts = q_ref.shape[1]
    bs = MOBA_BLOCK
    nbp = past_len // bs
    _page_scores(q_ref[0] * (DH_C ** -0.5), wmask_ref, kp_refs, kn_ref, s_sc, past_len)
    s = s_sc[...]
    blocks = [s[n * bs:(n + 1) * bs] for n in range(nbp)]
    g = jnp.concatenate([jnp.sum(b, axis=0, keepdims=True) for b in blocks], axis=0)
    selb = jnp.where(_topk_mask(g, jnp.full(g.shape, True), 0), 0.0, NEG)
    bias = bias_ref[...]
    parts = [blocks[n] + bias[n * bs:(n + 1) * bs] + selb[n:n + 1] for n in range(nbp)]
    parts.append(s[past_len:] + bias[past_len:])
    pr = _softmax_rows(jnp.concatenate(parts, axis=0))
    o_full = _page_values(pr, vp_refs, vn_ref, past_len)
    for h in range(H_C):
        o_ref[0, :, h * DH_C:(h + 1) * DH_C] = o_full[h * ts:(h + 1) * ts, h * DH_C:(h + 1) * DH_C]


def _paged_call(body, name, q, k_new, v_new, cache_k, cache_v, page_table, page_base, n_heads, n_maps, extra):
    b, ts, width = q.shape
    n_pages = page_table.shape[1]
    page = cache_k.shape[1]
    vwidth = cache_v.shape[2]
    past_len = n_pages * page
    assert n_heads * ts <= LANE and ts % 8 == 0
    ncol = n_maps * LANE
    lpad = past_len + LANE
    wmask = jnp.asarray(_score_cols_mask(ncol, n_heads, n_maps, width // (n_heads * n_maps), ts))
    bias = jnp.asarray(_score_bias(ncol, n_heads, ts, past_len, lpad))
    const = lambda a: pl.BlockSpec(a.shape, lambda bi, pt: (0,) * a.ndim)
    row = lambda w: pl.BlockSpec((1, ts, w), lambda bi, pt: (bi, 0, 0))
    page_spec = lambda w, g: pl.BlockSpec((1, page, w), lambda bi, pt: (page_base + pt[bi * n_pages + g], 0, 0))
    in_specs = [row(width), row(width), row(vwidth), const(wmask), const(bias)] + [const(a) for a in extra]
    in_specs += [page_spec(width, g) for g in range(n_pages)] + [page_spec(vwidth, g) for g in range(n_pages)]
    return pl.pallas_call(
        functools.partial(body, n_pages=n_pages, past_len=past_len),
        out_shape=jax.ShapeDtypeStruct((b, ts, vwidth), F32),
        grid_spec=pltpu.PrefetchScalarGridSpec(
            num_scalar_prefetch=1,
            grid=(b,),
            in_specs=in_specs,
            out_specs=row(vwidth),
            scratch_shapes=[pltpu.VMEM((lpad, ncol), F32)],
        ),
        compiler_params=_params("arbitrary"),
        name=name,
    )(page_table.reshape(-1), q, k_new, v_new, wmask, bias, *extra, *([cache_k] * n_pages), *([cache_v] * n_pages))


def _diff_sample(q, k_new, v_new, cache_k, cache_v, page_table, page_base, lvec, gn, lam_init):
    body = functools.partial(_diff_sample_kernel, lam_init=lam_init)
    return _paged_call(body, "diff_sample", q, k_new, v_new, cache_k, cache_v, page_table, page_base, H_B, 2,
                       [lvec, gn])


def _moba_sample(q, k_new, v_new, cache_k, cache_v, page_table, page_base):
    past_len = page_table.shape[1] * cache_k.shape[1]
    assert past_len % MOBA_BLOCK == 0 and past_len // MOBA_BLOCK >= MOBA_TOPK and q.shape[1] <= MOBA_BLOCK
    return _paged_call(_moba_sample_kernel, "moba_sample", q, k_new, v_new, cache_k, cache_v, page_table, page_base,
                       H_C, 1, [])


def _alibi_slopes(n):
    return jnp.array([2.0 ** (-8.0 * (h + 1) / n) for h in range(n)], dtype=F32)


def kernel(x_prompt, x_sample, c_prompt, c_sample, state_gla, cache_diff_k, cache_diff_v, cache_moba_k, cache_moba_v,
           page_table, norm_g, w_ada, b_ada, w_in_e, w_gla_gate, b_gla_gate, g_gla_norm, lam_q1, lam_k1, lam_q2,
           lam_k2, g_diff_norm, w_out_e, w_in_o, w_out_o, final_g):
    bp, tp, d = x_prompt.shape
    bs_, ts, _ = x_sample.shape
    depth = norm_g.shape[0]
    n_pool, page = cache_diff_k.shape[1], cache_diff_k.shape[2]
    assert d == D_MODEL

    tt_p = min(512, tp)
    nb_s = min(64, bs_)
    tq_diff = min(256, tp)

    mrows = bp + bs_
    mpad = -(-mrows // 8) * 8
    c_all = jnp.concatenate([c_prompt, c_sample, jnp.zeros((mpad - mrows, d), F32)], axis=0)
    mod = _ada(c_all, w_ada, b_ada)

    def mod_parts(l, lo, hi):
        m = mod[l, lo:hi]
        return m[:, None, 0:d], m[:, None, d:2 * d], m[:, None, 2 * d:3 * d]

    slopes_b = _alibi_slopes(H_B)
    slopes_c = _alibi_slopes(H_C)
    nqa = H_A * DK_A
    yp, ys = x_prompt, x_sample
    gla_p, gla_s, dk_p, dv_p, dk_s, dv_s, mk_p, mv_p, mk_s, mv_s = ([] for _ in range(10))
    for l in range(depth):
        shift_p, scale_p, gate_p = mod_parts(l, 0, bp)
        shift_s, scale_s, gate_s = mod_parts(l, bp, bp + bs_)
        g_l = norm_g[l].reshape(1, 1, d)
        last = l == depth - 1
        fg = final_g.reshape(1, 1, d) if last else None
        if l % 2 == 0:
            e = l // 2
            w = w_in_e[e]
            c0 = 2 * nqa + W_A
            c1 = c0 + GLA_LOWRANK
            w_cat = jnp.concatenate([w[:, :c0], w[:, c0:c1], jnp.zeros((d, LANE - GLA_LOWRANK), F32), w[:, c1:]],
                                    axis=1).astype(BF16)
            o0 = c0 + LANE
            splits = [(0, c0, 1.0), (c0, LANE, 1.0), (o0, 2 * H_B * DK_B, DK_B ** -0.5),
                      (o0 + 512, 512, 1.0), (o0 + 1024, 512, 1.0), (o0 + 1536, W_A + W_B, 1.0)]
            dtypes = [F32, F32, BF16, F32, F32, F32]
            wg = jnp.concatenate([w_gla_gate[e], jnp.zeros((LANE - GLA_LOWRANK, nqa), F32)], axis=0).astype(BF16)
            bg = b_gla_gate[e].reshape(1, nqa)
            gn_a = g_gla_norm[e].reshape(1, DV_A)
            gn_b = g_diff_norm[e].reshape(1, DV_B)
            lvec = jnp.stack([lam_q1[e], lam_k1[e], lam_q2[e], lam_k2[e]])
            lam_init = 0.8 - 0.6 * math.exp(-0.3 * l)
            w_out = w_out_e[e].astype(BF16)

            qkv_a, glow, q_b, k_b, v_b, gate = _inproj(yp, shift_p, scale_p, g_l, w_cat, splits, dtypes, 1, tt_p)
            o_a, s_new = _gla_prompt(qkv_a, glow, wg, bg, gn_a, tt_p)
            o_b = _diff_prompt(q_b, k_b, v_b, slopes_b, lvec, gn_b, lam_init, tq_diff)
            yp = _outproj(yp, [o_a, o_b], gate, gate_p, w_out, fg, 1, tt_p)
            gla_p.append(s_new)
            dk_p.append(k_b.reshape(bp, tp, H_B, 2 * DK_B))
            dv_p.append(v_b.reshape(bp, tp, H_B, DV_B))

            dtypes_s = [F32] * len(dtypes)
            qkv_a, glow, q_b, k_b, v_b, gate = _inproj(ys, shift_s, scale_s, g_l, w_cat, splits, dtypes_s, nb_s, ts)
            o_a, s_new = _gla_sample(qkv_a, glow, state_gla[e], wg, bg, gn_a, min(8, bs_))
            ck = cache_diff_k.reshape(-1, page, H_B * 2 * DK_B)
            cv = cache_diff_v.reshape(-1, page, W_B)
            o_b = _diff_sample(q_b, k_b, v_b, ck, cv, page_table, e * n_pool, lvec, gn_b, lam_init)
            ys = _outproj(ys, [o_a, o_b], gate, gate_s, w_out, fg, nb_s, ts)
            gla_s.append(s_new)
            dk_s.append(k_b.reshape(bs_, ts, H_B, 2 * DK_B))
            dv_s.append(v_b.reshape(bs_, ts, H_B, DV_B))
        else:
            o = l // 2
            w_cat = w_in_o[o].astype(BF16)
            splits = [(0, W_C, 1.0), (W_C, W_C, 1.0), (2 * W_C, W_C, 1.0), (3 * W_C, W_C, 1.0)]
            dtypes = [F32, F32, F32, F32]
            w_out = w_out_o[o].astype(BF16)

            q, k, v, gate = _inproj(yp, shift_p, scale_p, g_l, w_cat, splits, dtypes, 1, tt_p)
            att = _moba_prompt(q, k, v, slopes_c)
            yp = _outproj(yp, [att], gate, gate_p, w_out, fg, 1, tt_p)
            mk_p.append(k.reshape(bp, tp, H_C, DH_C))
            mv_p.append(v.reshape(bp, tp, H_C, DH_C))

            q, k, v, gate = _inproj(ys, shift_s, scale_s, g_l, w_cat, splits, dtypes, nb_s, ts)
            ck = cache_moba_k.reshape(-1, page, W_C)
            cv = cache_moba_v.reshape(-1, page, W_C)
            att = _moba_sample(q, k, v, ck, cv, page_table, o * n_pool)
            ys = _outproj(ys, [att], gate, gate_s, w_out, fg, nb_s, ts)
            mk_s.append(k.reshape(bs_, ts, H_C, DH_C))
            mv_s.append(v.reshape(bs_, ts, H_C, DH_C))
    return (yp, ys, jnp.stack(gla_p), jnp.stack(gla_s), jnp.stack(dk_p), jnp.stack(dv_p), jnp.stack(dk_s),
            jnp.stack(dv_s), jnp.stack(mk_p), jnp.stack(mv_p), jnp.stack(mk_s), jnp.stack(mv_s))
```

```python
import functools
import math

import jax
import jax.numpy as jnp
import numpy as np
from jax import lax
from jax.experimental import pallas as pl
from jax.experimental.pallas import tpu as pltpu

F32 = jnp.float32
BF16 = jnp.bfloat16

D_MODEL = 1024
H_A, DK_A, DV_A = 4, 64, 128
GLA_LOWRANK = 16
GLA_TAU = 16.0
GLA_CHUNK = 64
H_B, DK_B, DV_B = 4, 64, 128
H_C, DH_C = 8, 128
MOBA_BLOCK = 256
MOBA_TOPK = 3
W_A = H_A * DV_A
W_B = H_B * DV_B
W_C = H_C * DH_C
EPS = 1e-6
NEG = -1e30
LANE = 128
VT_ROWS = LANE + 16
KV_GROUP = 4
VMEM_LIMIT = 56 * 1024 * 1024

HIGHEST = lax.Precision.HIGHEST


def _dot(a, b):
    return jnp.dot(a, b, preferred_element_type=F32)


def _dot_nt(a, b, precision=None):
    return lax.dot_general(a, b, (((1,), (1,)), ((), ())), preferred_element_type=F32, precision=precision)


def _dot_tn(a, b):
    return lax.dot_general(a, b, (((0,), (0,)), ((), ())), preferred_element_type=F32)


def _rms(x, g):
    return x * lax.rsqrt(jnp.mean(x * x, axis=-1, keepdims=True) + EPS) * g


def _silu(x):
    return x * jax.nn.sigmoid(x)


def _params(*sem):
    return pltpu.CompilerParams(dimension_semantics=sem, vmem_limit_bytes=VMEM_LIMIT)


def _ada_kernel(c_ref, w_ref, b_ref, o_ref):
    a = _silu(c_ref[...]).astype(BF16)
    o_ref[0] = _dot(a, w_ref[0].astype(BF16)) + b_ref[0]


def _ada(c_all, w_ada, b_ada):
    depth = w_ada.shape[0]
    mp = c_all.shape[0]
    tn = 1024
    return pl.pallas_call(
        _ada_kernel,
        out_shape=jax.ShapeDtypeStruct((depth, mp, 3 * D_MODEL), F32),
        grid=(depth, 3 * D_MODEL // tn),
        in_specs=[
            pl.BlockSpec((mp, D_MODEL), lambda l, j: (0, 0)),
            pl.BlockSpec((1, D_MODEL, tn), lambda l, j: (l, 0, j)),
            pl.BlockSpec((1, 1, tn), lambda l, j: (l, 0, j)),
        ],
        out_specs=pl.BlockSpec((1, mp, tn), lambda l, j: (l, 0, j)),
        compiler_params=_params("arbitrary", "arbitrary"),
        name="ada_mod",
    )(c_all, w_ada, b_ada.reshape(depth, 1, 3 * D_MODEL))


def _inproj_kernel(x_ref, shift_ref, scale_ref, g_ref, w_ref, *out_refs, splits):
    nb, tt, d = x_ref.shape
    x = x_ref[...]
    h = _rms(x, g_ref[...]) * (1.0 + scale_ref[...]) + shift_ref[...]
    h2 = h.reshape(nb * tt, d).astype(BF16)
    head_refs = list(out_refs[len(splits):])
    for (start, width, mult, n_heads), o_ref in zip(splits, out_refs):
        z = _dot(h2, w_ref[:, start:start + width])
        if mult != 1.0:
            z = z * mult
        o_ref[...] = z.reshape(nb, tt, width).astype(o_ref.dtype)
        if n_heads:
            hr = head_refs.pop(0)
            for hd in range(n_heads):
                hr.at[0][pl.ds(hd, tt, stride=n_heads), :] = z[:, hd * LANE:(hd + 1) * LANE]


def _inproj(x, shift, scale, g, w, splits, dtypes, nb, tt):
    nseq, t, d = x.shape
    n = w.shape[1]
    grid = (nseq // nb, t // tt)
    xmap = lambda i, j: (i, j, 0)
    out_shape = [jax.ShapeDtypeStruct((nseq, t, wd), dt) for (_, wd, _, _), dt in zip(splits, dtypes)]
    out_specs = [pl.BlockSpec((nb, tt, wd), xmap) for (_, wd, _, _) in splits]
    for (_, wd, _, nh) in splits:
        if nh:
            assert nb == 1 and wd == nh * LANE
            out_shape.append(jax.ShapeDtypeStruct((nseq, t * nh, LANE), F32))
            out_specs.append(pl.BlockSpec((1, tt * nh, LANE), xmap))
    return pl.pallas_call(
        functools.partial(_inproj_kernel, splits=tuple(splits)),
        out_shape=out_shape,
        grid=grid,
        in_specs=[
            pl.BlockSpec((nb, tt, d), xmap),
            pl.BlockSpec((nb, 1, d), lambda i, j: (i, 0, 0)),
            pl.BlockSpec((nb, 1, d), lambda i, j: (i, 0, 0)),
            pl.BlockSpec((1, 1, d), lambda i, j: (0, 0, 0)),
            pl.BlockSpec((d, n), lambda i, j: (0, 0)),
        ],
        out_specs=out_specs,
        compiler_params=_params("arbitrary", "arbitrary"),
        name="norm_inproj",
    )(x, shift, scale, g, w)


def _outproj_kernel(*refs, n_parts, final):
    x_ref = refs[0]
    o_refs = refs[1:1 + n_parts]
    gate_ref, ag_ref, w_ref = refs[1 + n_parts:4 + n_parts]
    rest = refs[4 + n_parts:]
    fg_ref = rest[0] if final else None
    y_ref = rest[-1]
    nb, tt, d = x_ref.shape
    rows = nb * tt
    sg = _silu(gate_ref[...].reshape(rows, -1))
    acc = jnp.zeros((rows, d), F32)
    col = 0
    for o_ref in o_refs:
        wd = o_ref.shape[-1]
        u = (o_ref[...].reshape(rows, wd) * sg[:, col:col + wd]).astype(BF16)
        acc = acc + _dot(u, w_ref[col:col + wd, :])
        col += wd
    y = x_ref[...] + ag_ref[...] * acc.reshape(nb, tt, d)
    if final:
        y = _rms(y, fg_ref[...])
    y_ref[...] = y


def _outproj(x, o_parts, gate, ada_gate, w, final_g, nb, tt):
    nseq, t, d = x.shape
    grid = (nseq // nb, t // tt)
    xmap = lambda i, j: (i, j, 0)
    final = final_g is not None
    in_specs = [pl.BlockSpec((nb, tt, d), xmap)]
    in_specs += [pl.BlockSpec((nb, tt, o.shape[-1]), xmap) for o in o_parts]
    in_specs += [
        pl.BlockSpec((nb, tt, gate.shape[-1]), xmap),
        pl.BlockSpec((nb, 1, d), lambda i, j: (i, 0, 0)),
        pl.BlockSpec(w.shape, lambda i, j: (0, 0)),
    ]
    args = [x, *o_parts, gate, ada_gate, w]
    if final:
        in_specs.append(pl.BlockSpec((1, 1, d), lambda i, j: (0, 0, 0)))
        args.append(final_g)
    return pl.pallas_call(
        functools.partial(_outproj_kernel, n_parts=len(o_parts), final=final),
        out_shape=jax.ShapeDtypeStruct((nseq, t, d), F32),
        grid=grid,
        in_specs=in_specs,
        out_specs=pl.BlockSpec((nb, tt, d), xmap),
        compiler_params=_params("arbitrary", "arbitrary"),
        name="gate_outproj",
    )(*args)


def _gla_rows(q, k, v, glow, wg, bg, gnorm, states, seq_len):
    r = q.shape[0]
    ns = r // seq_len
    la = jax.nn.log_sigmoid(_dot(glow.astype(BF16), wg) + bg) / GLA_TAU
    ri = lax.broadcasted_iota(jnp.int32, (r, r), 0)
    ci = lax.broadcasted_iota(jnp.int32, (r, r), 1)
    same = (ri // seq_len) == (ci // seq_len)
    causal = same & (ci <= ri)
    cum = jnp.dot(causal.astype(F32), la, preferred_element_type=F32, precision=HIGHEST)
    tot = jnp.dot(same.astype(F32), la, preferred_element_type=F32, precision=HIGHEST)
    q_all = q * (DK_A ** -0.5) * jnp.exp(cum)
    k_all = k * jnp.exp(-cum)
    ke_all = k * jnp.exp(tot - cum)
    dec = jnp.exp(tot)
    er = lax.broadcasted_iota(jnp.int32, (DK_A, DK_A), 0)
    ec = lax.broadcasted_iota(jnp.int32, (DK_A, DK_A), 1)
    eye = (er == ec).astype(F32)
    outs = []
    new_states = [[None] * H_A for _ in range(ns)]
    for h in range(H_A):
        ks = slice(h * DK_A, (h + 1) * DK_A)
        vs = slice(h * DV_A, (h + 1) * DV_A)
        q_in = q_all[:, ks]
        k_end = ke_all[:, ks]
        v_h = v[:, vs]
        att = jnp.where(causal, _dot_nt(q_in.astype(BF16), k_all[:, ks].astype(BF16)), 0.0)
        o = _dot(att.astype(BF16), v_h.astype(BF16))
        o_rows = []
        for j in range(ns):
            rs = slice(j * seq_len, (j + 1) * seq_len)
            s_old = states[j][h]
            o_rows.append(o[rs] + _dot(q_in[rs].astype(BF16), s_old.astype(BF16)))
            d_col = jnp.sum(eye * dec[j * seq_len:j * seq_len + 1, ks], axis=1, keepdims=True)
            new_states[j][h] = s_old * d_col + _dot_tn(k_end[rs].astype(BF16), v_h[rs].astype(BF16))
        o = o_rows[0] if ns == 1 else jnp.concatenate(o_rows, axis=0)
        outs.append(_rms(o, gnorm))
    return outs, new_states


def _gla_prompt_kernel(qkv_ref, glow_ref, wg_ref, bg_ref, gn_ref, o_ref, s_ref, st_sc):
    t = pl.program_id(1)
    tt = qkv_ref.shape[1]

    @pl.when(t == 0)
    def _():
        st_sc[...] = jnp.zeros_like(st_sc)

    wg = wg_ref[...]
    bg = bg_ref[...]
    gn = gn_ref[...]
    nq = H_A * DK_A

    def chunk(c, carry):
        r0 = pl.multiple_of(c * GLA_CHUNK, GLA_CHUNK)
        rows = pl.ds(r0, GLA_CHUNK)
        q = qkv_ref[0, rows, 0:nq]
        k = qkv_ref[0, rows, nq:2 * nq]
        v = qkv_ref[0, rows, 2 * nq:2 * nq + W_A]
        states = [[st_sc[h] for h in range(H_A)]]
        outs, new_states = _gla_rows(q, k, v, glow_ref[0, rows, :], wg, bg, gn, states, GLA_CHUNK)
        for h in range(H_A):
            o_ref[0, rows, h * DV_A:(h + 1) * DV_A] = outs[h]
            st_sc[h] = new_states[0][h]
        return carry

    lax.fori_loop(0, tt // GLA_CHUNK, chunk, 0)

    @pl.when(t == pl.num_programs(1) - 1)
    def _():
        s_ref[0] = st_sc[...]


def _gla_prompt(qkv, glow, wg, bg, gn, tt):
    b, t, _ = qkv.shape
    assert t % tt == 0 and tt % GLA_CHUNK == 0
    return pl.pallas_call(
        _gla_prompt_kernel,
        out_shape=[jax.ShapeDtypeStruct((b, t, W_A), F32), jax.ShapeDtypeStruct((b, H_A, DK_A, DV_A), F32)],
        grid=(b, t // tt),
        in_specs=[
            pl.BlockSpec((1, tt, qkv.shape[-1]), lambda i, j: (i, j, 0)),
            pl.BlockSpec((1, tt, LANE), lambda i, j: (i, j, 0)),
            pl.BlockSpec(wg.shape, lambda i, j: (0, 0)),
            pl.BlockSpec(bg.shape, lambda i, j: (0, 0)),
            pl.BlockSpec(gn.shape, lambda i, j: (0, 0)),
        ],
        out_specs=[
            pl.BlockSpec((1, tt, W_A), lambda i, j: (i, j, 0)),
            pl.BlockSpec((1, H_A, DK_A, DV_A), lambda i, j: (i, 0, 0, 0)),
        ],
        scratch_shapes=[pltpu.VMEM((H_A, DK_A, DV_A), F32)],
        compiler_params=_params("arbitrary", "arbitrary"),
        name="gla_prompt",
    )(qkv, glow, wg, bg, gn)


def _gla_sample_kernel(qkv_ref, glow_ref, s0_ref, wg_ref, bg_ref, gn_ref, o_ref, s_ref):
    nb, ts, _ = qkv_ref.shape
    nq = H_A * DK_A
    r = nb * ts
    qkv = qkv_ref[...].reshape(r, qkv_ref.shape[-1])
    glow = glow_ref[...].reshape(r, LANE)
    states = [[s0_ref[j, h] for h in range(H_A)] for j in range(nb)]
    outs, new_states = _gla_rows(qkv[:, 0:nq], qkv[:, nq:2 * nq], qkv[:, 2 * nq:2 * nq + W_A], glow,
                                 wg_ref[...], bg_ref[...], gn_ref[...], states, ts)
    for h in range(H_A):
        o_ref[:, :, h * DV_A:(h + 1) * DV_A] = outs[h].reshape(nb, ts, DV_A)
        for j in range(nb):
            s_ref[j, h] = new_states[j][h]


def _gla_sample(qkv, glow, s0, wg, bg, gn, nb):
    b, ts, _ = qkv.shape
    assert b % nb == 0 and ts % 8 == 0 and ts <= GLA_CHUNK
    return pl.pallas_call(
        _gla_sample_kernel,
        out_shape=[jax.ShapeDtypeStruct((b, ts, W_A), F32), jax.ShapeDtypeStruct((b, H_A, DK_A, DV_A), F32)],
        grid=(b // nb,),
        in_specs=[
            pl.BlockSpec((nb, ts, qkv.shape[-1]), lambda i: (i, 0, 0)),
            pl.BlockSpec((nb, ts, LANE), lambda i: (i, 0, 0)),
            pl.BlockSpec((nb, H_A, DK_A, DV_A), lambda i: (i, 0, 0, 0)),
            pl.BlockSpec(wg.shape, lambda i: (0, 0)),
            pl.BlockSpec(bg.shape, lambda i: (0, 0)),
            pl.BlockSpec(gn.shape, lambda i: (0, 0)),
        ],
        out_specs=[
            pl.BlockSpec((nb, ts, W_A), lambda i: (i, 0, 0)),
            pl.BlockSpec((nb, H_A, DK_A, DV_A), lambda i: (i, 0, 0, 0)),
        ],
        compiler_params=_params("arbitrary"),
        name="gla_sample",
    )(qkv, glow, s0, wg, bg, gn)


def _attend_cols(kaug_sc, vt_sc, q_aug, i, tq, acc_sc):
    ncol = q_aug.shape[0]
    t = kaug_sc.shape[0]
    gk = min(KV_GROUP * tq, t)
    kr = lax.broadcasted_iota(jnp.int32, (gk, ncol), 0)
    qc = lax.broadcasted_iota(jnp.int32, (gk, ncol), 1) % tq
    for v in range(t // gk):
        head = v * gk

        @pl.when((i * tq) // gk == v)
        def _(head=head):
            s_t = _dot_nt(kaug_sc[head:head + gk, :], q_aug)
            s_t = jnp.where(kr + head <= qc + i * tq, s_t, NEG)
            m = jnp.max(s_t, axis=0, keepdims=True)
            if head > 0:
                s_h = _dot_nt(kaug_sc[0:head, :], q_aug)
                m = jnp.maximum(m, jnp.max(s_h, axis=0, keepdims=True))
            acc = _dot(vt_sc[:, head:head + gk], jnp.exp(s_t - m).astype(BF16))
            if head > 0:
                acc = acc + _dot(vt_sc[:, 0:head], jnp.exp(s_h - m).astype(BF16))
            acc_sc[...] = acc


def _fill_kv(k_ref, v_ref, kaug_sc, vt_sc, tk, aug_fn, km_sc=None):
    nblk = kaug_sc.shape[0] // tk
    dv = v_ref.shape[-1]
    rowv = lax.broadcasted_iota(jnp.int32, (VT_ROWS - dv, tk), 0)
    ones_rows = jnp.where(rowv == 0, 1.0, 0.0).astype(BF16)
    for n in range(nblk):
        rows = slice(n * tk, (n + 1) * tk)
        kt = k_ref[0, rows, :]
        if km_sc is not None:
            km_sc[n:n + 1, :] = jnp.mean(kt, axis=0, keepdims=True)
        kaug_sc[rows, 0:LANE] = kt.astype(BF16)
        kaug_sc[rows, LANE:] = aug_fn(n).astype(BF16)
        vt_sc[0:dv, rows] = v_ref[0, rows, :].T.astype(BF16)
        vt_sc[dv:, rows] = ones_rows


def _lambda(l_ref, lam_init):
    lv = l_ref[...]
    s1 = jnp.sum(lv[0:1] * lv[1:2], axis=1, keepdims=True)
    s2 = jnp.sum(lv[2:3] * lv[3:4], axis=1, keepdims=True)
    return jnp.exp(s1) - jnp.exp(s2) + lam_init


def _diff_prompt_kernel(slope_ref, q_ref, k_ref, v_ref, l_ref, gn_ref, o_ref, kaug_sc, vt_sc, acc_sc, *, lam_init):
    h = pl.program_id(1)
    i = pl.program_id(2)
    tq = q_ref.shape[1]

    @pl.when(i == 0)
    def _():
        slope = slope_ref[h]
        c = lax.broadcasted_iota(jnp.int32, (tq, LANE), 0).astype(F32)
        col = lax.broadcasted_iota(jnp.int32, (tq, LANE), 1)

        def aug(n):
            return jnp.where(col == 0, slope * float(tq * n), jnp.where(col == 1, slope * c, 0.0))

        _fill_kv(k_ref, v_ref, kaug_sc, vt_sc, tq, aug)

    q = q_ref[0]
    lane = lax.broadcasted_iota(jnp.int32, q.shape, 1)
    zero = jnp.zeros_like(q)
    ones2 = jnp.where(lane < 2, 1.0, 0.0).astype(BF16)
    q_aug = jnp.concatenate([
        jnp.concatenate([jnp.where(lane < DK_B, q, zero), ones2], axis=1),
        jnp.concatenate([jnp.where(lane >= DK_B, q, zero), ones2], axis=1)], axis=0)
    _attend_cols(kaug_sc, vt_sc, q_aug, i, tq, acc_sc)
    acc = acc_sc[...]
    o_t = acc[0:DV_B] / acc[DV_B:DV_B + 1]
    lam = _lambda(l_ref, lam_init)
    o = (o_t[:, :tq] - lam * o_t[:, tq:]).T
    o_ref[0] = _rms(o, gn_ref[...]) * (1.0 - lam_init)


def _diff_prompt(q, k, v, slopes, lvec, gn, lam_init, tq):
    b, t, _ = q.shape
    assert t % min(KV_GROUP * tq, t) == 0
    nblk = t // tq
    kv_spec = pl.BlockSpec((1, t, LANE), lambda bi, h, i, s: (bi, 0, h))
    return pl.pallas_call(
        functools.partial(_diff_prompt_kernel, lam_init=lam_init),
        out_shape=jax.ShapeDtypeStruct((b, t, W_B), F32),
        grid_spec=pltpu.PrefetchScalarGridSpec(
            num_scalar_prefetch=1,
            grid=(b, H_B, nblk),
            in_specs=[
                pl.BlockSpec((1, tq, LANE), lambda bi, h, i, s: (bi, i, h)),
                kv_spec,
                kv_spec,
                pl.BlockSpec(lvec.shape, lambda bi, h, i, s: (0, 0)),
                pl.BlockSpec(gn.shape, lambda bi, h, i, s: (0, 0)),
            ],
            out_specs=pl.BlockSpec((1, tq, LANE), lambda bi, h, i, s: (bi, i, h)),
            scratch_shapes=[
                pltpu.VMEM((t, 2 * LANE), BF16),
                pltpu.VMEM((VT_ROWS, t), BF16),
                pltpu.VMEM((VT_ROWS, 2 * tq), F32),
            ],
        ),
        compiler_params=_params("arbitrary", "arbitrary", "arbitrary"),
        name="diff_prompt",
    )(slopes, q, k, v, lvec, gn)


def _topk_mask(g, valid, n_axis):
    nb = g.shape[n_axis]
    g = jnp.where(valid, g, -jnp.inf)
    idx = lax.broadcasted_iota(jnp.int32, g.shape, n_axis)
    rank = jnp.zeros(g.shape, jnp.int32)
    for m in range(nb):
        gm = lax.slice_in_dim(g, m, m + 1, axis=n_axis)
        beats = (gm > g) | ((gm == g) & (m < idx))
        rank = rank + beats.astype(jnp.int32)
    return (rank < MOBA_TOPK) & valid


def _moba_prompt_kernel(slope_ref, q_ref, k_ref, v_ref, o_ref, km_sc, kaug_sc, vt_sc, acc_sc):
    h = pl.program_id(1)
    i = pl.program_id(2)
    bs = q_ref.shape[1]
    nblk = km_sc.shape[0]

    @pl.when(i == 0)
    def _():
        slope = slope_ref[h]
        c = lax.broadcasted_iota(jnp.int32, (bs, LANE), 0).astype(F32)
        col = lax.broadcasted_iota(jnp.int32, (bs, LANE), 1)

        def aug(n):
            return jnp.where(col == n, 1.0, jnp.where(col == nblk, slope * float(bs * n),
                                                      jnp.where(col == nblk + 1, slope * c, 0.0)))

        _fill_kv(k_ref, v_ref, kaug_sc, vt_sc, bs, aug, km_sc)

    q = q_ref[0]
    g = _dot_nt(km_sc[...], q, precision=HIGHEST)
    blk = lax.broadcasted_iota(jnp.int32, g.shape, 0)
    sel = _topk_mask(g, blk < i, 0)
    selb = jnp.where(sel | (blk == i), 0.0, NEG)
    row = lax.broadcasted_iota(jnp.int32, (LANE, bs), 0)
    coef = jnp.concatenate([selb, jnp.zeros((LANE - nblk, bs), F32)], axis=0)
    coef = jnp.where((row == nblk) | (row == nblk + 1), 1.0, coef)
    q_aug = jnp.concatenate([q * (DH_C ** -0.5), coef.T], axis=1).astype(BF16)
    _attend_cols(kaug_sc, vt_sc, q_aug, i, bs, acc_sc)
    acc = acc_sc[...]
    o_ref[0] = (acc[0:DH_C] / acc[DH_C:DH_C + 1]).T


def _moba_prompt(q, k, v, slopes):
    b, t, _ = q.shape
    bs = MOBA_BLOCK
    assert t % min(KV_GROUP * bs, t) == 0
    nblk = t // bs
    assert nblk + 2 <= LANE
    kv_spec = pl.BlockSpec((1, t, LANE), lambda bi, h, i, s: (bi, 0, h))
    q_spec = pl.BlockSpec((1, bs, LANE), lambda bi, h, i, s: (bi, i, h))
    return pl.pallas_call(
        _moba_prompt_kernel,
        out_shape=jax.ShapeDtypeStruct((b, t, W_C), F32),
        grid_spec=pltpu.PrefetchScalarGridSpec(
            num_scalar_prefetch=1,
            grid=(b, H_C, nblk),
            in_specs=[q_spec, kv_spec, kv_spec],
            out_specs=q_spec,
            scratch_shapes=[
                pltpu.VMEM((nblk, DH_C), F32),
                pltpu.VMEM((t, 2 * LANE), BF16),
                pltpu.VMEM((VT_ROWS, t), BF16),
                pltpu.VMEM((VT_ROWS, bs), F32),
            ],
        ),
        compiler_params=_params("arbitrary", "arbitrary", "arbitrary"),
        name="moba_prompt",
    )(slopes, q, k, v)


def _score_cols_mask(n_col, n_heads, n_maps, dk, ts):
    c = np.arange(n_col)[:, None]
    f = np.arange(n_heads * n_maps * dk)[None, :]
    return (((c % LANE) // ts == f // (n_maps * dk)) & (c // LANE == (f % (n_maps * dk)) // dk)).astype(np.float32)


def _score_bias(n_col, n_heads, ts, past_len, lpad):
    j = np.arange(lpad)[:, None]
    c = np.arange(n_col)[None, :]
    hh = (c % LANE) // ts
    rel = past_len + c % ts - j
    slope = np.array([2.0 ** (-8.0 * (h + 1) / n_heads) for h in range(LANE // ts + 1)], np.float32)[hh]
    bias = np.where(hh < n_heads, -slope * rel.astype(np.float32), np.float32(0.0))
    return np.where((rel >= 0) & (j < past_len + ts), bias, np.float32(NEG)).astype(np.float32)


def _head_pair(p_ref, j, n_heads, page):
    rows = p_ref.at[0]
    return jnp.concatenate([rows[pl.ds(2 * j, page, stride=n_heads), :],
                            rows[pl.ds(2 * j + 1, page, stride=n_heads), :]], axis=1).astype(BF16)


def _page_scores(q, wmask_ref, kp_refs, kn_ref, s_sc, past_len, n_heads):
    ts, width = q.shape
    ncol = s_sc.shape[1]
    page = kp_refs[0].shape[1] // n_heads
    wt = (jnp.broadcast_to(q[None], (ncol // ts, ts, width)).reshape(ncol, width) * wmask_ref[...]).astype(BF16)
    for g, kp_ref in enumerate(kp_refs):
        acc = None
        for j in range(n_heads // 2):
            part = _dot_nt(_head_pair(kp_ref, j, n_heads, page), wt[:, 2 * j * LANE:(2 * j + 2) * LANE])
            acc = part if acc is None else acc + part
        s_sc[g * page:(g + 1) * page, :] = acc
    s_sc[past_len:past_len + ts, :] = _dot_nt(kn_ref[0].astype(BF16), wt)
    s_sc[past_len + ts:, :] = jnp.zeros((s_sc.shape[0] - past_len - ts, ncol), F32)


def _page_values(a_t, vp_refs, vn_ref, past_len, n_heads):
    page = vp_refs[0].shape[1] // n_heads
    ts, width = vn_ref.shape[1], vn_ref.shape[2]
    acc = [jnp.zeros((2 * ts, 2 * LANE), F32) for _ in range(n_heads // 2)]
    for g, vp_ref in enumerate(vp_refs):
        at = a_t[g * page:(g + 1) * page].T
        for j in range(n_heads // 2):
            acc[j] = acc[j] + _dot(at[2 * ts * j:2 * ts * (j + 1)].astype(BF16), _head_pair(vp_ref, j, n_heads, page))
    v_tail = jnp.concatenate([vn_ref[0], jnp.zeros((LANE - ts, width), F32)], axis=0)
    o_tail = _dot(a_t[past_len:].T.astype(BF16), v_tail.astype(BF16))
    outs = []
    for h in range(n_heads):
        j, r = divmod(h, 2)
        outs.append(acc[j][r * ts:(r + 1) * ts, r * LANE:(r + 1) * LANE]
                    + o_tail[h * ts:(h + 1) * ts, h * LANE:(h + 1) * LANE])
    return outs


def _softmax_rows(s):
    e = jnp.exp(s - jnp.max(s, axis=0, keepdims=True))
    return e * (1.0 / jnp.sum(e, axis=0, keepdims=True))


def _diff_sample_kernel(pt_ref, q_ref, kn_ref, vn_ref, wmask_ref, bias_ref, l_ref, gn_ref, *rest,
                        n_pages, lam_init, past_len):
    kp_refs, vp_refs = rest[:n_pages], rest[n_pages:2 * n_pages]
    o_ref, s_sc = rest[2 * n_pages:]
    ts = q_ref.shape[1]
    _page_scores(q_ref[0], wmask_ref, kp_refs, kn_ref, s_sc, past_len, H_B)
    pr = _softmax_rows(s_sc[...] + bias_ref[...])
    lam = _lambda(l_ref, lam_init)
    a_t = pr[:, :LANE] - lam * pr[:, LANE:]
    gn = gn_ref[...]
    for h, o in enumerate(_page_values(a_t, vp_refs, vn_ref, past_len, H_B)):
        o_ref[0, :, h * DV_B:(h + 1) * DV_B] = _rms(o, gn) * (1.0 - lam_init)


def _moba_sample_kernel(pt_ref, q_ref, kn_ref, vn_ref, wmask_ref, bias_ref, *rest, n_pages, past_len):
    kp_refs, vp_refs = rest[:n_pages], rest[n_pages:2 * n_pages]
    o_ref, s_sc = rest[2 * n_pages:]
    ts = q_ref.shape[1]
    bs = MOBA_BLOCK
    nbp = past_len // bs
    _page_scores(q_ref[0] * (DH_C ** -0.5), wmask_ref, kp_refs, kn_ref, s_sc, past_len, H_C)
    s = s_sc[...]
    blocks = [s[n * bs:(n + 1) * bs] for n in range(nbp)]
    g = jnp.concatenate([jnp.sum(b, axis=0, keepdims=True) for b in blocks], axis=0)
    selb = jnp.where(_topk_mask(g, jnp.full(g.shape, True), 0), 0.0, NEG)
    bias = bias_ref[...]
    parts = [blocks[n] + bias[n * bs:(n + 1) * bs] + selb[n:n + 1] for n in range(nbp)]
    parts.append(s[past_len:] + bias[past_len:])
    pr = _softmax_rows(jnp.concatenate(parts, axis=0))
    for h, o in enumerate(_page_values(pr, vp_refs, vn_ref, past_len, H_C)):
        o_ref[0, :, h * DH_C:(h + 1) * DH_C] = o


def _paged_call(body, name, q, k_new, v_new, cache_k, cache_v, page_table, page_base, n_heads, n_maps, extra):
    b, ts, width = q.shape
    n_pages = page_table.shape[1]
    page = cache_k.shape[1] // n_heads
    vwidth = v_new.shape[2]
    past_len = n_pages * page
    assert n_heads * ts <= LANE and ts % 8 == 0 and n_heads % 2 == 0
    assert width == n_heads * LANE and vwidth == n_heads * LANE and cache_k.shape[2] == LANE
    ncol = n_maps * LANE
    lpad = past_len + LANE
    wmask = jnp.asarray(_score_cols_mask(ncol, n_heads, n_maps, width // (n_heads * n_maps), ts))
    bias = jnp.asarray(_score_bias(ncol, n_heads, ts, past_len, lpad))
    const = lambda a: pl.BlockSpec(a.shape, lambda bi, pt: (0,) * a.ndim)
    row = lambda w: pl.BlockSpec((1, ts, w), lambda bi, pt: (bi, 0, 0))
    page_spec = lambda g: pl.BlockSpec((1, page * n_heads, LANE),
                                       lambda bi, pt: (page_base + pt[bi * n_pages + g], 0, 0))
    in_specs = [row(width), row(width), row(vwidth), const(wmask), const(bias)] + [const(a) for a in extra]
    in_specs += [page_spec(g) for g in range(n_pages)] * 2
    return pl.pallas_call(
        functools.partial(body, n_pages=n_pages, past_len=past_len),
        out_shape=jax.ShapeDtypeStruct((b, ts, vwidth), F32),
        grid_spec=pltpu.PrefetchScalarGridSpec(
            num_scalar_prefetch=1,
            grid=(b,),
            in_specs=in_specs,
            out_specs=row(vwidth),
            scratch_shapes=[pltpu.VMEM((lpad, ncol), F32)],
        ),
        compiler_params=_params("arbitrary"),
        name=name,
    )(page_table.reshape(-1), q, k_new, v_new, wmask, bias, *extra, *([cache_k] * n_pages), *([cache_v] * n_pages))


def _diff_sample(q, k_new, v_new, cache_k, cache_v, page_table, page_base, lvec, gn, lam_init):
    body = functools.partial(_diff_sample_kernel, lam_init=lam_init)
    return _paged_call(body, "diff_sample", q, k_new, v_new, cache_k, cache_v, page_table, page_base, H_B, 2,
                       [lvec, gn])


def _moba_sample(q, k_new, v_new, cache_k, cache_v, page_table, page_base):
    past_len = page_table.shape[1] * cache_k.shape[1] // H_C
    assert past_len % MOBA_BLOCK == 0 and past_len // MOBA_BLOCK >= MOBA_TOPK and q.shape[1] <= MOBA_BLOCK
    return _paged_call(_moba_sample_kernel, "moba_sample", q, k_new, v_new, cache_k, cache_v, page_table, page_base,
                       H_C, 1, [])


def _alibi_slopes(n):
    return jnp.array([2.0 ** (-8.0 * (h + 1) / n) for h in range(n)], dtype=F32)


def kernel(x_prompt, x_sample, c_prompt, c_sample, state_gla, cache_diff_k, cache_diff_v, cache_moba_k, cache_moba_v,
           page_table, norm_g, w_ada, b_ada, w_in_e, w_gla_gate, b_gla_gate, g_gla_norm, lam_q1, lam_k1, lam_q2,
           lam_k2, g_diff_norm, w_out_e, w_in_o, w_out_o, final_g):
    bp, tp, d = x_prompt.shape
    bs_, ts, _ = x_sample.shape
    depth = norm_g.shape[0]
    n_pool, page = cache_diff_k.shape[1], cache_diff_k.shape[2]
    assert d == D_MODEL

    tt_p = min(512, tp)
    nb_s = min(64, bs_)
    tq_diff = min(256, tp)

    mrows = bp + bs_
    mpad = -(-mrows // 8) * 8
    c_all = jnp.concatenate([c_prompt, c_sample, jnp.zeros((mpad - mrows, d), F32)], axis=0)
    mod = _ada(c_all, w_ada, b_ada)

    def mod_parts(l, lo, hi):
        m = mod[l, lo:hi]
        return m[:, None, 0:d], m[:, None, d:2 * d], m[:, None, 2 * d:3 * d]

    slopes_b = _alibi_slopes(H_B)
    slopes_c = _alibi_slopes(H_C)
    nqa = H_A * DK_A
    yp, ys = x_prompt, x_sample
    gla_p, gla_s, dk_p, dv_p, dk_s, dv_s, mk_p, mv_p, mk_s, mv_s = ([] for _ in range(10))
    for l in range(depth):
        shift_p, scale_p, gate_p = mod_parts(l, 0, bp)
        shift_s, scale_s, gate_s = mod_parts(l, bp, bp + bs_)
        g_l = norm_g[l].reshape(1, 1, d)
        last = l == depth - 1
        fg = final_g.reshape(1, 1, d) if last else None
        if l % 2 == 0:
            e = l // 2
            w = w_in_e[e]
            c0 = 2 * nqa + W_A
            c1 = c0 + GLA_LOWRANK
            w_cat = jnp.concatenate([w[:, :c0], w[:, c0:c1], jnp.zeros((d, LANE - GLA_LOWRANK), F32), w[:, c1:]],
                                    axis=1).astype(BF16)
            o0 = c0 + LANE
            splits = [(0, c0, 1.0, 0), (c0, LANE, 1.0, 0), (o0, 2 * H_B * DK_B, DK_B ** -0.5, 0),
                      (o0 + 512, 512, 1.0, H_B), (o0 + 1024, 512, 1.0, H_B), (o0 + 1536, W_A + W_B, 1.0, 0)]
            splits_s = [sp[:3] + (0,) for sp in splits]
            dtypes = [F32, F32, BF16, F32, F32, F32]
            wg = jnp.concatenate([w_gla_gate[e], jnp.zeros((LANE - GLA_LOWRANK, nqa), F32)], axis=0).astype(BF16)
            bg = b_gla_gate[e].reshape(1, nqa)
            gn_a = g_gla_norm[e].reshape(1, DV_A)
            gn_b = g_diff_norm[e].reshape(1, DV_B)
            lvec = jnp.stack([lam_q1[e], lam_k1[e], lam_q2[e], lam_k2[e]])
            lam_init = 0.8 - 0.6 * math.exp(-0.3 * l)
            w_out = w_out_e[e].astype(BF16)

            qkv_a, glow, q_b, k_b, v_b, gate, k_hd, v_hd = _inproj(yp, shift_p, scale_p, g_l, w_cat, splits, dtypes,
                                                                   1, tt_p)
            o_a, s_new = _gla_prompt(qkv_a, glow, wg, bg, gn_a, tt_p)
            o_b = _diff_prompt(q_b, k_b, v_b, slopes_b, lvec, gn_b, lam_init, tq_diff)
            yp = _outproj(yp, [o_a, o_b], gate, gate_p, w_out, fg, 1, tt_p)
            gla_p.append(s_new)
            dk_p.append(k_hd.reshape(bp, tp, H_B, 2 * DK_B))
            dv_p.append(v_hd.reshape(bp, tp, H_B, DV_B))

            dtypes_s = [F32] * len(dtypes)
            qkv_a, glow, q_b, k_b, v_b, gate = _inproj(ys, shift_s, scale_s, g_l, w_cat, splits_s, dtypes_s, nb_s, ts)
            o_a, s_new = _gla_sample(qkv_a, glow, state_gla[e], wg, bg, gn_a, min(8, bs_))
            ck = cache_diff_k.reshape(-1, page * H_B, 2 * DK_B)
            cv = cache_diff_v.reshape(-1, page * H_B, DV_B)
            o_b = _diff_sample(q_b, k_b, v_b, ck, cv, page_table, e * n_pool, lvec, gn_b, lam_init)
            ys = _outproj(ys, [o_a, o_b], gate, gate_s, w_out, fg, nb_s, ts)
            gla_s.append(s_new)
            dk_s.append(k_b.reshape(bs_, ts, H_B, 2 * DK_B))
            dv_s.append(v_b.reshape(bs_, ts, H_B, DV_B))
        else:
            o = l // 2
            w_cat = w_in_o[o].astype(BF16)
            splits = [(0, W_C, 1.0, 0), (W_C, W_C, 1.0, H_C), (2 * W_C, W_C, 1.0, H_C), (3 * W_C, W_C, 1.0, 0)]
            splits_s = [sp[:3] + (0,) for sp in splits]
            dtypes = [F32, F32, F32, F32]
            w_out = w_out_o[o].astype(BF16)

            q, k, v, gate, k_hd, v_hd = _inproj(yp, shift_p, scale_p, g_l, w_cat, splits, dtypes, 1, tt_p)
            att = _moba_prompt(q, k, v, slopes_c)
            yp = _outproj(yp, [att], gate, gate_p, w_out, fg, 1, tt_p)
            mk_p.append(k_hd.reshape(bp, tp, H_C, DH_C))
            mv_p.append(v_hd.reshape(bp, tp, H_C, DH_C))

            q, k, v, gate = _inproj(ys, shift_s, scale_s, g_l, w_cat, splits_s, dtypes, nb_s, ts)
            ck = cache_moba_k.reshape(-1, page * H_C, DH_C)
            cv = cache_moba_v.reshape(-1, page * H_C, DH_C)
            att = _moba_sample(q, k, v, ck, cv, page_table, o * n_pool)
            ys = _outproj(ys, [att], gate, gate_s, w_out, fg, nb_s, ts)
            mk_s.append(k.reshape(bs_, ts, H_C, DH_C))
            mv_s.append(v.reshape(bs_, ts, H_C, DH_C))
    return (yp, ys, jnp.stack(gla_p), jnp.stack(gla_s), jnp.stack(dk_p), jnp.stack(dv_p), jnp.stack(dk_s),
            jnp.stack(dv_s), jnp.stack(mk_p), jnp.stack(mv_p), jnp.stack(mk_s), jnp.stack(mv_s))
```

```python
import functools
import math

import jax
import jax.numpy as jnp
import numpy as np
from jax import lax
from jax.experimental import pallas as pl
from jax.experimental.pallas import tpu as pltpu

F32 = jnp.float32
BF16 = jnp.bfloat16

D_MODEL = 1024
H_A, DK_A, DV_A = 4, 64, 128
GLA_LOWRANK = 16
GLA_TAU = 16.0
GLA_CHUNK = 64
H_B, DK_B, DV_B = 4, 64, 128
H_C, DH_C = 8, 128
MOBA_BLOCK = 256
MOBA_TOPK = 3
W_A = H_A * DV_A
W_B = H_B * DV_B
W_C = H_C * DH_C
EPS = 1e-6
NEG = -1e30
LANE = 128
VT_ROWS = LANE + 16
KV_GROUP_KEYS = 1024
MOBA_Q_TILE = 512
VMEM_LIMIT = 56 * 1024 * 1024

HIGHEST = lax.Precision.HIGHEST


def _dot(a, b):
    return jnp.dot(a, b, preferred_element_type=F32)


def _dot_nt(a, b, precision=None):
    return lax.dot_general(a, b, (((1,), (1,)), ((), ())), preferred_element_type=F32, precision=precision)


def _dot_tn(a, b):
    return lax.dot_general(a, b, (((0,), (0,)), ((), ())), preferred_element_type=F32)


def _rms(x, g):
    return x * lax.rsqrt(jnp.mean(x * x, axis=-1, keepdims=True) + EPS) * g


def _silu(x):
    return x * jax.nn.sigmoid(x)


def _params(*sem):
    return pltpu.CompilerParams(dimension_semantics=sem, vmem_limit_bytes=VMEM_LIMIT)


def _ada_kernel(c_ref, w_ref, b_ref, o_ref):
    a = _silu(c_ref[...]).astype(BF16)
    o_ref[0] = _dot(a, w_ref[0].astype(BF16)) + b_ref[0]


def _ada(c_all, w_ada, b_ada):
    depth = w_ada.shape[0]
    mp = c_all.shape[0]
    tn = 1024
    return pl.pallas_call(
        _ada_kernel,
        out_shape=jax.ShapeDtypeStruct((depth, mp, 3 * D_MODEL), F32),
        grid=(depth, 3 * D_MODEL // tn),
        in_specs=[
            pl.BlockSpec((mp, D_MODEL), lambda l, j: (0, 0)),
            pl.BlockSpec((1, D_MODEL, tn), lambda l, j: (l, 0, j)),
            pl.BlockSpec((1, 1, tn), lambda l, j: (l, 0, j)),
        ],
        out_specs=pl.BlockSpec((1, mp, tn), lambda l, j: (l, 0, j)),
        compiler_params=_params("arbitrary", "arbitrary"),
        name="ada_mod",
    )(c_all, w_ada, b_ada.reshape(depth, 1, 3 * D_MODEL))


def _inproj_kernel(x_ref, shift_ref, scale_ref, g_ref, w_ref, *out_refs, splits):
    nb, tt, d = x_ref.shape
    x = x_ref[...]
    h = _rms(x, g_ref[...]) * (1.0 + scale_ref[...]) + shift_ref[...]
    h2 = h.reshape(nb * tt, d).astype(BF16)
    head_refs = list(out_refs[len(splits):])
    for (start, width, mult, n_heads), o_ref in zip(splits, out_refs):
        z = _dot(h2, w_ref[:, start:start + width])
        if mult != 1.0:
            z = z * mult
        o_ref[...] = z.reshape(nb, tt, width).astype(o_ref.dtype)
        if n_heads:
            hr = head_refs.pop(0)
            for hd in range(n_heads):
                hr.at[0][pl.ds(hd, tt, stride=n_heads), :] = z[:, hd * LANE:(hd + 1) * LANE]


def _inproj(x, shift, scale, g, w, splits, dtypes, nb, tt):
    nseq, t, d = x.shape
    n = w.shape[1]
    grid = (nseq // nb, t // tt)
    xmap = lambda i, j: (i, j, 0)
    out_shape = [jax.ShapeDtypeStruct((nseq, t, wd), dt) for (_, wd, _, _), dt in zip(splits, dtypes)]
    out_specs = [pl.BlockSpec((nb, tt, wd), xmap) for (_, wd, _, _) in splits]
    for (_, wd, _, nh) in splits:
        if nh:
            assert nb == 1 and wd == nh * LANE
            out_shape.append(jax.ShapeDtypeStruct((nseq, t * nh, LANE), F32))
            out_specs.append(pl.BlockSpec((1, tt * nh, LANE), xmap))
    return pl.pallas_call(
        functools.partial(_inproj_kernel, splits=tuple(splits)),
        out_shape=out_shape,
        grid=grid,
        in_specs=[
            pl.BlockSpec((nb, tt, d), xmap),
            pl.BlockSpec((nb, 1, d), lambda i, j: (i, 0, 0)),
            pl.BlockSpec((nb, 1, d), lambda i, j: (i, 0, 0)),
            pl.BlockSpec((1, 1, d), lambda i, j: (0, 0, 0)),
            pl.BlockSpec((d, n), lambda i, j: (0, 0)),
        ],
        out_specs=out_specs,
        compiler_params=_params("arbitrary", "arbitrary"),
        name="norm_inproj",
    )(x, shift, scale, g, w)


def _outproj_kernel(*refs, n_parts, final):
    x_ref = refs[0]
    o_refs = refs[1:1 + n_parts]
    gate_ref, ag_ref, w_ref = refs[1 + n_parts:4 + n_parts]
    rest = refs[4 + n_parts:]
    fg_ref = rest[0] if final else None
    y_ref = rest[-1]
    nb, tt, d = x_ref.shape
    rows = nb * tt
    sg = _silu(gate_ref[...].reshape(rows, -1))
    acc = jnp.zeros((rows, d), F32)
    col = 0
    for o_ref in o_refs:
        wd = o_ref.shape[-1]
        u = (o_ref[...].reshape(rows, wd) * sg[:, col:col + wd]).astype(BF16)
        acc = acc + _dot(u, w_ref[col:col + wd, :])
        col += wd
    y = x_ref[...] + ag_ref[...] * acc.reshape(nb, tt, d)
    if final:
        y = _rms(y, fg_ref[...])
    y_ref[...] = y


def _outproj(x, o_parts, gate, ada_gate, w, final_g, nb, tt):
    nseq, t, d = x.shape
    grid = (nseq // nb, t // tt)
    xmap = lambda i, j: (i, j, 0)
    final = final_g is not None
    in_specs = [pl.BlockSpec((nb, tt, d), xmap)]
    in_specs += [pl.BlockSpec((nb, tt, o.shape[-1]), xmap) for o in o_parts]
    in_specs += [
        pl.BlockSpec((nb, tt, gate.shape[-1]), xmap),
        pl.BlockSpec((nb, 1, d), lambda i, j: (i, 0, 0)),
        pl.BlockSpec(w.shape, lambda i, j: (0, 0)),
    ]
    args = [x, *o_parts, gate, ada_gate, w]
    if final:
        in_specs.append(pl.BlockSpec((1, 1, d), lambda i, j: (0, 0, 0)))
        args.append(final_g)
    return pl.pallas_call(
        functools.partial(_outproj_kernel, n_parts=len(o_parts), final=final),
        out_shape=jax.ShapeDtypeStruct((nseq, t, d), F32),
        grid=grid,
        in_specs=in_specs,
        out_specs=pl.BlockSpec((nb, tt, d), xmap),
        compiler_params=_params("arbitrary", "arbitrary"),
        name="gate_outproj",
    )(*args)


def _gla_rows(q, k, v, glow, wg, bg, gnorm, states, seq_len):
    r = q.shape[0]
    ns = r // seq_len
    la = jax.nn.log_sigmoid(_dot(glow.astype(BF16), wg) + bg) / GLA_TAU
    ri = lax.broadcasted_iota(jnp.int32, (r, r), 0)
    ci = lax.broadcasted_iota(jnp.int32, (r, r), 1)
    same = (ri // seq_len) == (ci // seq_len)
    causal = same & (ci <= ri)
    cum = jnp.dot(causal.astype(F32), la, preferred_element_type=F32, precision=HIGHEST)
    tot = jnp.dot(same.astype(F32), la, preferred_element_type=F32, precision=HIGHEST)
    q_all = q * (DK_A ** -0.5) * jnp.exp(cum)
    k_all = k * jnp.exp(-cum)
    ke_all = k * jnp.exp(tot - cum)
    dec = jnp.exp(tot)
    er = lax.broadcasted_iota(jnp.int32, (DK_A, DK_A), 0)
    ec = lax.broadcasted_iota(jnp.int32, (DK_A, DK_A), 1)
    eye = (er == ec).astype(F32)
    outs = []
    new_states = [[None] * H_A for _ in range(ns)]
    for h in range(H_A):
        ks = slice(h * DK_A, (h + 1) * DK_A)
        vs = slice(h * DV_A, (h + 1) * DV_A)
        q_in = q_all[:, ks]
        k_end = ke_all[:, ks]
        v_h = v[:, vs]
        att = jnp.where(causal, _dot_nt(q_in.astype(BF16), k_all[:, ks].astype(BF16)), 0.0)
        o = _dot(att.astype(BF16), v_h.astype(BF16))
        o_rows = []
        for j in range(ns):
            rs = slice(j * seq_len, (j + 1) * seq_len)
            s_old = states[j][h]
            o_rows.append(o[rs] + _dot(q_in[rs].astype(BF16), s_old.astype(BF16)))
            d_col = jnp.sum(eye * dec[j * seq_len:j * seq_len + 1, ks], axis=1, keepdims=True)
            new_states[j][h] = s_old * d_col + _dot_tn(k_end[rs].astype(BF16), v_h[rs].astype(BF16))
        o = o_rows[0] if ns == 1 else jnp.concatenate(o_rows, axis=0)
        outs.append(_rms(o, gnorm))
    return outs, new_states


def _gla_prompt_kernel(qkv_ref, glow_ref, wg_ref, bg_ref, gn_ref, o_ref, s_ref, st_sc):
    t = pl.program_id(1)
    tt = qkv_ref.shape[1]

    @pl.when(t == 0)
    def _():
        st_sc[...] = jnp.zeros_like(st_sc)

    wg = wg_ref[...]
    bg = bg_ref[...]
    gn = gn_ref[...]
    nq = H_A * DK_A

    def chunk(c, carry):
        r0 = pl.multiple_of(c * GLA_CHUNK, GLA_CHUNK)
        rows = pl.ds(r0, GLA_CHUNK)
        q = qkv_ref[0, rows, 0:nq]
        k = qkv_ref[0, rows, nq:2 * nq]
        v = qkv_ref[0, rows, 2 * nq:2 * nq + W_A]
        states = [[st_sc[h] for h in range(H_A)]]
        outs, new_states = _gla_rows(q, k, v, glow_ref[0, rows, :], wg, bg, gn, states, GLA_CHUNK)
        for h in range(H_A):
            o_ref[0, rows, h * DV_A:(h + 1) * DV_A] = outs[h]
            st_sc[h] = new_states[0][h]
        return carry

    lax.fori_loop(0, tt // GLA_CHUNK, chunk, 0)

    @pl.when(t == pl.num_programs(1) - 1)
    def _():
        s_ref[0] = st_sc[...]


def _gla_prompt(qkv, glow, wg, bg, gn, tt):
    b, t, _ = qkv.shape
    assert t % tt == 0 and tt % GLA_CHUNK == 0
    return pl.pallas_call(
        _gla_prompt_kernel,
        out_shape=[jax.ShapeDtypeStruct((b, t, W_A), F32), jax.ShapeDtypeStruct((b, H_A, DK_A, DV_A), F32)],
        grid=(b, t // tt),
        in_specs=[
            pl.BlockSpec((1, tt, qkv.shape[-1]), lambda i, j: (i, j, 0)),
            pl.BlockSpec((1, tt, LANE), lambda i, j: (i, j, 0)),
            pl.BlockSpec(wg.shape, lambda i, j: (0, 0)),
            pl.BlockSpec(bg.shape, lambda i, j: (0, 0)),
            pl.BlockSpec(gn.shape, lambda i, j: (0, 0)),
        ],
        out_specs=[
            pl.BlockSpec((1, tt, W_A), lambda i, j: (i, j, 0)),
            pl.BlockSpec((1, H_A, DK_A, DV_A), lambda i, j: (i, 0, 0, 0)),
        ],
        scratch_shapes=[pltpu.VMEM((H_A, DK_A, DV_A), F32)],
        compiler_params=_params("arbitrary", "arbitrary"),
        name="gla_prompt",
    )(qkv, glow, wg, bg, gn)


def _gla_sample_kernel(qkv_ref, glow_ref, s0_ref, wg_ref, bg_ref, gn_ref, o_ref, s_ref):
    nb, ts, _ = qkv_ref.shape
    nq = H_A * DK_A
    r = nb * ts
    qkv = qkv_ref[...].reshape(r, qkv_ref.shape[-1])
    glow = glow_ref[...].reshape(r, LANE)
    states = [[s0_ref[j, h] for h in range(H_A)] for j in range(nb)]
    outs, new_states = _gla_rows(qkv[:, 0:nq], qkv[:, nq:2 * nq], qkv[:, 2 * nq:2 * nq + W_A], glow,
                                 wg_ref[...], bg_ref[...], gn_ref[...], states, ts)
    for h in range(H_A):
        o_ref[:, :, h * DV_A:(h + 1) * DV_A] = outs[h].reshape(nb, ts, DV_A)
        for j in range(nb):
            s_ref[j, h] = new_states[j][h]


def _gla_sample(qkv, glow, s0, wg, bg, gn, nb):
    b, ts, _ = qkv.shape
    assert b % nb == 0 and ts % 8 == 0 and ts <= GLA_CHUNK
    return pl.pallas_call(
        _gla_sample_kernel,
        out_shape=[jax.ShapeDtypeStruct((b, ts, W_A), F32), jax.ShapeDtypeStruct((b, H_A, DK_A, DV_A), F32)],
        grid=(b // nb,),
        in_specs=[
            pl.BlockSpec((nb, ts, qkv.shape[-1]), lambda i: (i, 0, 0)),
            pl.BlockSpec((nb, ts, LANE), lambda i: (i, 0, 0)),
            pl.BlockSpec((nb, H_A, DK_A, DV_A), lambda i: (i, 0, 0, 0)),
            pl.BlockSpec(wg.shape, lambda i: (0, 0)),
            pl.BlockSpec(bg.shape, lambda i: (0, 0)),
            pl.BlockSpec(gn.shape, lambda i: (0, 0)),
        ],
        out_specs=[
            pl.BlockSpec((nb, ts, W_A), lambda i: (i, 0, 0)),
            pl.BlockSpec((nb, H_A, DK_A, DV_A), lambda i: (i, 0, 0, 0)),
        ],
        compiler_params=_params("arbitrary"),
        name="gla_sample",
    )(qkv, glow, s0, wg, bg, gn)


def _attend_cols(kaug_sc, vt_sc, q_aug, i, tq, acc_sc):
    ncol = q_aug.shape[0]
    t = kaug_sc.shape[0]
    gk = min(KV_GROUP_KEYS, t)
    assert gk % tq == 0 and t % gk == 0
    kr = lax.broadcasted_iota(jnp.int32, (gk, ncol), 0)
    qc = lax.broadcasted_iota(jnp.int32, (gk, ncol), 1) % tq
    for v in range(t // gk):
        head = v * gk

        @pl.when((i * tq) // gk == v)
        def _(head=head):
            s_t = _dot_nt(kaug_sc[head:head + gk, :], q_aug)
            s_t = jnp.where(kr + head <= qc + i * tq, s_t, NEG)
            m = jnp.max(s_t, axis=0, keepdims=True)
            if head > 0:
                s_h = _dot_nt(kaug_sc[0:head, :], q_aug)
                m = jnp.maximum(m, jnp.max(s_h, axis=0, keepdims=True))
            acc = _dot(vt_sc[:, head:head + gk], jnp.exp(s_t - m).astype(BF16))
            if head > 0:
                acc = acc + _dot(vt_sc[:, 0:head], jnp.exp(s_h - m).astype(BF16))
            acc_sc[...] = acc


def _fill_kv(k_ref, v_ref, kaug_sc, vt_sc, tk, aug_fn, km_sc=None):
    nblk = kaug_sc.shape[0] // tk
    dv = v_ref.shape[-1]
    rowv = lax.broadcasted_iota(jnp.int32, (VT_ROWS - dv, tk), 0)
    ones_rows = jnp.where(rowv == 0, 1.0, 0.0).astype(BF16)
    for n in range(nblk):
        rows = slice(n * tk, (n + 1) * tk)
        kt = k_ref[0, rows, :]
        if km_sc is not None:
            km_sc[n:n + 1, :] = jnp.mean(kt, axis=0, keepdims=True)
        kaug_sc[rows, 0:LANE] = kt.astype(BF16)
        kaug_sc[rows, LANE:] = aug_fn(n).astype(BF16)
        vt_sc[0:dv, rows] = v_ref[0, rows, :].T.astype(BF16)
        vt_sc[dv:, rows] = ones_rows


def _lambda(l_ref, lam_init):
    lv = l_ref[...]
    s1 = jnp.sum(lv[0:1] * lv[1:2], axis=1, keepdims=True)
    s2 = jnp.sum(lv[2:3] * lv[3:4], axis=1, keepdims=True)
    return jnp.exp(s1) - jnp.exp(s2) + lam_init


def _diff_prompt_kernel(slope_ref, q_ref, k_ref, v_ref, l_ref, gn_ref, o_ref, kaug_sc, vt_sc, acc_sc, *, lam_init):
    h = pl.program_id(1)
    i = pl.program_id(2)
    tq = q_ref.shape[1]

    @pl.when(i == 0)
    def _():
        slope = slope_ref[h]
        c = lax.broadcasted_iota(jnp.int32, (tq, LANE), 0).astype(F32)
        col = lax.broadcasted_iota(jnp.int32, (tq, LANE), 1)

        def aug(n):
            return jnp.where(col == 0, slope * float(tq * n), jnp.where(col == 1, slope * c, 0.0))

        _fill_kv(k_ref, v_ref, kaug_sc, vt_sc, tq, aug)

    q = q_ref[0]
    lane = lax.broadcasted_iota(jnp.int32, q.shape, 1)
    zero = jnp.zeros_like(q)
    ones2 = jnp.where(lane < 2, 1.0, 0.0).astype(BF16)
    q_aug = jnp.concatenate([
        jnp.concatenate([jnp.where(lane < DK_B, q, zero), ones2], axis=1),
        jnp.concatenate([jnp.where(lane >= DK_B, q, zero), ones2], axis=1)], axis=0)
    _attend_cols(kaug_sc, vt_sc, q_aug, i, tq, acc_sc)
    acc = acc_sc[...]
    o_t = acc[0:DV_B] / acc[DV_B:DV_B + 1]
    lam = _lambda(l_ref, lam_init)
    o = (o_t[:, :tq] - lam * o_t[:, tq:]).T
    o_ref[0] = _rms(o, gn_ref[...]) * (1.0 - lam_init)


def _diff_prompt(q, k, v, slopes, lvec, gn, lam_init, tq):
    b, t, _ = q.shape
    assert t % tq == 0
    nblk = t // tq
    kv_spec = pl.BlockSpec((1, t, LANE), lambda bi, h, i, s: (bi, 0, h))
    return pl.pallas_call(
        functools.partial(_diff_prompt_kernel, lam_init=lam_init),
        out_shape=jax.ShapeDtypeStruct((b, t, W_B), F32),
        grid_spec=pltpu.PrefetchScalarGridSpec(
            num_scalar_prefetch=1,
            grid=(b, H_B, nblk),
            in_specs=[
                pl.BlockSpec((1, tq, LANE), lambda bi, h, i, s: (bi, i, h)),
                kv_spec,
                kv_spec,
                pl.BlockSpec(lvec.shape, lambda bi, h, i, s: (0, 0)),
                pl.BlockSpec(gn.shape, lambda bi, h, i, s: (0, 0)),
            ],
            out_specs=pl.BlockSpec((1, tq, LANE), lambda bi, h, i, s: (bi, i, h)),
            scratch_shapes=[
                pltpu.VMEM((t, 2 * LANE), BF16),
                pltpu.VMEM((VT_ROWS, t), BF16),
                pltpu.VMEM((VT_ROWS, 2 * tq), F32),
            ],
        ),
        compiler_params=_params("arbitrary", "arbitrary", "arbitrary"),
        name="diff_prompt",
    )(slopes, q, k, v, lvec, gn)


def _topk_mask(g, valid, n_axis):
    nb = g.shape[n_axis]
    g = jnp.where(valid, g, -jnp.inf)
    idx = lax.broadcasted_iota(jnp.int32, g.shape, n_axis)
    rank = jnp.zeros(g.shape, jnp.int32)
    for m in range(nb):
        gm = lax.slice_in_dim(g, m, m + 1, axis=n_axis)
        beats = (gm > g) | ((gm == g) & (m < idx))
        rank = rank + beats.astype(jnp.int32)
    return (rank < MOBA_TOPK) & valid


def _moba_prompt_kernel(slope_ref, q_ref, k_ref, v_ref, o_ref, km_sc, kaug_sc, vt_sc, acc_sc):
    h = pl.program_id(1)
    i = pl.program_id(2)
    tq = q_ref.shape[1]
    bs = MOBA_BLOCK
    nblk = km_sc.shape[0]

    @pl.when(i == 0)
    def _():
        slope = slope_ref[h]
        c = lax.broadcasted_iota(jnp.int32, (bs, LANE), 0).astype(F32)
        col = lax.broadcasted_iota(jnp.int32, (bs, LANE), 1)

        def aug(n):
            return jnp.where(col == n, 1.0, jnp.where(col == nblk, slope * float(bs * n),
                                                      jnp.where(col == nblk + 1, slope * c, 0.0)))

        _fill_kv(k_ref, v_ref, kaug_sc, vt_sc, bs, aug, km_sc)

    q = q_ref[0]
    g = _dot_nt(km_sc[...], q, precision=HIGHEST)
    blk = lax.broadcasted_iota(jnp.int32, g.shape, 0)
    own = i * (tq // bs) + lax.broadcasted_iota(jnp.int32, g.shape, 1) // bs
    sel = _topk_mask(g, blk < own, 0)
    selb = jnp.where(sel | (blk == own), 0.0, NEG)
    row = lax.broadcasted_iota(jnp.int32, (LANE, tq), 0)
    coef = jnp.concatenate([selb, jnp.zeros((LANE - nblk, tq), F32)], axis=0)
    coef = jnp.where((row == nblk) | (row == nblk + 1), 1.0, coef)
    q_aug = jnp.concatenate([q * (DH_C ** -0.5), coef.T], axis=1).astype(BF16)
    _attend_cols(kaug_sc, vt_sc, q_aug, i, tq, acc_sc)
    acc = acc_sc[...]
    o_ref[0] = (acc[0:DH_C] / acc[DH_C:DH_C + 1]).T


def _moba_prompt(q, k, v, slopes):
    b, t, _ = q.shape
    bs = MOBA_BLOCK
    tq = min(MOBA_Q_TILE, t)
    assert t % tq == 0 and tq % bs == 0
    nblk = t // bs
    assert nblk + 2 <= LANE
    kv_spec = pl.BlockSpec((1, t, LANE), lambda bi, h, i, s: (bi, 0, h))
    q_spec = pl.BlockSpec((1, tq, LANE), lambda bi, h, i, s: (bi, i, h))
    return pl.pallas_call(
        _moba_prompt_kernel,
        out_shape=jax.ShapeDtypeStruct((b, t, W_C), F32),
        grid_spec=pltpu.PrefetchScalarGridSpec(
            num_scalar_prefetch=1,
            grid=(b, H_C, t // tq),
            in_specs=[q_spec, kv_spec, kv_spec],
            out_specs=q_spec,
            scratch_shapes=[
                pltpu.VMEM((nblk, DH_C), F32),
                pltpu.VMEM((t, 2 * LANE), BF16),
                pltpu.VMEM((VT_ROWS, t), BF16),
                pltpu.VMEM((VT_ROWS, tq), F32),
            ],
        ),
        compiler_params=_params("arbitrary", "arbitrary", "arbitrary"),
        name="moba_prompt",
    )(slopes, q, k, v)


def _score_cols_mask(n_col, n_heads, n_maps, dk, ts):
    c = np.arange(n_col)[:, None]
    f = np.arange(n_heads * n_maps * dk)[None, :]
    return (((c % LANE) // ts == f // (n_maps * dk)) & (c // LANE == (f % (n_maps * dk)) // dk)).astype(np.float32)


def _score_bias(n_col, n_heads, ts, past_len, lpad):
    j = np.arange(lpad)[:, None]
    c = np.arange(n_col)[None, :]
    hh = (c % LANE) // ts
    rel = past_len + c % ts - j
    slope = np.array([2.0 ** (-8.0 * (h + 1) / n_heads) for h in range(LANE // ts + 1)], np.float32)[hh]
    bias = np.where(hh < n_heads, -slope * rel.astype(np.float32), np.float32(0.0))
    return np.where((rel >= 0) & (j < past_len + ts), bias, np.float32(NEG)).astype(np.float32)


def _head_pair(p_ref, j, n_heads, page):
    rows = p_ref.at[0]
    return jnp.concatenate([rows[pl.ds(2 * j, page, stride=n_heads), :],
                            rows[pl.ds(2 * j + 1, page, stride=n_heads), :]], axis=1).astype(BF16)


def _page_scores(q, wmask_ref, kp_refs, kn_ref, s_sc, past_len, n_heads):
    ts, width = q.shape
    ncol = s_sc.shape[1]
    page = kp_refs[0].shape[1] // n_heads
    wt = (jnp.broadcast_to(q[None], (ncol // ts, ts, width)).reshape(ncol, width) * wmask_ref[...]).astype(BF16)
    for g, kp_ref in enumerate(kp_refs):
        acc = None
        for j in range(n_heads // 2):
            part = _dot_nt(_head_pair(kp_ref, j, n_heads, page), wt[:, 2 * j * LANE:(2 * j + 2) * LANE])
            acc = part if acc is None else acc + part
        s_sc[g * page:(g + 1) * page, :] = acc
    s_sc[past_len:past_len + ts, :] = _dot_nt(kn_ref[0].astype(BF16), wt)
    s_sc[past_len + ts:, :] = jnp.zeros((s_sc.shape[0] - past_len - ts, ncol), F32)


def _page_values(a_t, vp_refs, vn_ref, past_len, n_heads):
    page = vp_refs[0].shape[1] // n_heads
    ts, width = vn_ref.shape[1], vn_ref.shape[2]
    acc = [jnp.zeros((2 * ts, 2 * LANE), F32) for _ in range(n_heads // 2)]
    for g, vp_ref in enumerate(vp_refs):
        at = a_t[g * page:(g + 1) * page].T
        for j in range(n_heads // 2):
            acc[j] = acc[j] + _dot(at[2 * ts * j:2 * ts * (j + 1)].astype(BF16), _head_pair(vp_ref, j, n_heads, page))
    v_tail = jnp.concatenate([vn_ref[0], jnp.zeros((LANE - ts, width), F32)], axis=0)
    o_tail = _dot(a_t[past_len:].T.astype(BF16), v_tail.astype(BF16))
    outs = []
    for h in range(n_heads):
        j, r = divmod(h, 2)
        outs.append(acc[j][r * ts:(r + 1) * ts, r * LANE:(r + 1) * LANE]
                    + o_tail[h * ts:(h + 1) * ts, h * LANE:(h + 1) * LANE])
    return outs


def _softmax_rows(s):
    e = jnp.exp(s - jnp.max(s, axis=0, keepdims=True))
    return e * (1.0 / jnp.sum(e, axis=0, keepdims=True))


def _diff_sample_kernel(pt_ref, q_ref, kn_ref, vn_ref, wmask_ref, bias_ref, l_ref, gn_ref, *rest,
                        n_pages, lam_init, past_len):
    kp_refs, vp_refs = rest[:n_pages], rest[n_pages:2 * n_pages]
    o_ref, s_sc = rest[2 * n_pages:]
    ts = q_ref.shape[1]
    _page_scores(q_ref[0], wmask_ref, kp_refs, kn_ref, s_sc, past_len, H_B)
    pr = _softmax_rows(s_sc[...] + bias_ref[...])
    lam = _lambda(l_ref, lam_init)
    a_t = pr[:, :LANE] - lam * pr[:, LANE:]
    gn = gn_ref[...]
    for h, o in enumerate(_page_values(a_t, vp_refs, vn_ref, past_len, H_B)):
        o_ref[0, :, h * DV_B:(h + 1) * DV_B] = _rms(o, gn) * (1.0 - lam_init)


def _moba_sample_kernel(pt_ref, q_ref, kn_ref, vn_ref, wmask_ref, bias_ref, *rest, n_pages, past_len):
    kp_refs, vp_refs = rest[:n_pages], rest[n_pages:2 * n_pages]
    o_ref, s_sc = rest[2 * n_pages:]
    ts = q_ref.shape[1]
    bs = MOBA_BLOCK
    nbp = past_len // bs
    _page_scores(q_ref[0] * (DH_C ** -0.5), wmask_ref, kp_refs, kn_ref, s_sc, past_len, H_C)
    s = s_sc[...]
    blocks = [s[n * bs:(n + 1) * bs] for n in range(nbp)]
    g = jnp.concatenate([jnp.sum(b, axis=0, keepdims=True) for b in blocks], axis=0)
    selb = jnp.where(_topk_mask(g, jnp.full(g.shape, True), 0), 0.0, NEG)
    bias = bias_ref[...]
    parts = [blocks[n] + bias[n * bs:(n + 1) * bs] + selb[n:n + 1] for n in range(nbp)]
    parts.append(s[past_len:] + bias[past_len:])
    pr = _softmax_rows(jnp.concatenate(parts, axis=0))
    for h, o in enumerate(_page_values(pr, vp_refs, vn_ref, past_len, H_C)):
        o_ref[0, :, h * DH_C:(h + 1) * DH_C] = o


def _paged_call(body, name, q, k_new, v_new, cache_k, cache_v, page_table, page_base, n_heads, n_maps, extra):
    b, ts, width = q.shape
    n_pages = page_table.shape[1]
    page = cache_k.shape[1] // n_heads
    vwidth = v_new.shape[2]
    past_len = n_pages * page
    assert n_heads * ts <= LANE and ts % 8 == 0 and n_heads % 2 == 0
    assert width == n_heads * LANE and vwidth == n_heads * LANE and cache_k.shape[2] == LANE
    ncol = n_maps * LANE
    lpad = past_len + LANE
    wmask = jnp.asarray(_score_cols_mask(ncol, n_heads, n_maps, width // (n_heads * n_maps), ts))
    bias = jnp.asarray(_score_bias(ncol, n_heads, ts, past_len, lpad))
    const = lambda a: pl.BlockSpec(a.shape, lambda bi, pt: (0,) * a.ndim)
    row = lambda w: pl.BlockSpec((1, ts, w), lambda bi, pt: (bi, 0, 0))
    page_spec = lambda g: pl.BlockSpec((1, page * n_heads, LANE),
                                       lambda bi, pt: (page_base + pt[bi * n_pages + g], 0, 0))
    in_specs = [row(width), row(width), row(vwidth), const(wmask), const(bias)] + [const(a) for a in extra]
    in_specs += [page_spec(g) for g in range(n_pages)] * 2
    return pl.pallas_call(
        functools.partial(body, n_pages=n_pages, past_len=past_len),
        out_shape=jax.ShapeDtypeStruct((b, ts, vwidth), F32),
        grid_spec=pltpu.PrefetchScalarGridSpec(
            num_scalar_prefetch=1,
            grid=(b,),
            in_specs=in_specs,
            out_specs=row(vwidth),
            scratch_shapes=[pltpu.VMEM((lpad, ncol), F32)],
        ),
        compiler_params=_params("arbitrary"),
        name=name,
    )(page_table.reshape(-1), q, k_new, v_new, wmask, bias, *extra, *([cache_k] * n_pages), *([cache_v] * n_pages))


def _diff_sample(q, k_new, v_new, cache_k, cache_v, page_table, page_base, lvec, gn, lam_init):
    body = functools.partial(_diff_sample_kernel, lam_init=lam_init)
    return _paged_call(body, "diff_sample", q, k_new, v_new, cache_k, cache_v, page_table, page_base, H_B, 2,
                       [lvec, gn])


def _moba_sample(q, k_new, v_new, cache_k, cache_v, page_table, page_base):
    past_len = page_table.shape[1] * cache_k.shape[1] // H_C
    assert past_len % MOBA_BLOCK == 0 and past_len // MOBA_BLOCK >= MOBA_TOPK and q.shape[1] <= MOBA_BLOCK
    return _paged_call(_moba_sample_kernel, "moba_sample", q, k_new, v_new, cache_k, cache_v, page_table, page_base,
                       H_C, 1, [])


def _alibi_slopes(n):
    return jnp.array([2.0 ** (-8.0 * (h + 1) / n) for h in range(n)], dtype=F32)


def kernel(x_prompt, x_sample, c_prompt, c_sample, state_gla, cache_diff_k, cache_diff_v, cache_moba_k, cache_moba_v,
           page_table, norm_g, w_ada, b_ada, w_in_e, w_gla_gate, b_gla_gate, g_gla_norm, lam_q1, lam_k1, lam_q2,
           lam_k2, g_diff_norm, w_out_e, w_in_o, w_out_o, final_g):
    bp, tp, d = x_prompt.shape
    bs_, ts, _ = x_sample.shape
    depth = norm_g.shape[0]
    n_pool, page = cache_diff_k.shape[1], cache_diff_k.shape[2]
    assert d == D_MODEL

    tt_p = min(512, tp)
    nb_s = min(64, bs_)
    tq_diff = min(256, tp)

    mrows = bp + bs_
    mpad = -(-mrows // 8) * 8
    c_all = jnp.concatenate([c_prompt, c_sample, jnp.zeros((mpad - mrows, d), F32)], axis=0)
    mod = _ada(c_all, w_ada, b_ada)

    def mod_parts(l, lo, hi):
        m = mod[l, lo:hi]
        return m[:, None, 0:d], m[:, None, d:2 * d], m[:, None, 2 * d:3 * d]

    slopes_b = _alibi_slopes(H_B)
    slopes_c = _alibi_slopes(H_C)
    nqa = H_A * DK_A
    yp, ys = x_prompt, x_sample
    gla_p, gla_s, dk_p, dv_p, dk_s, dv_s, mk_p, mv_p, mk_s, mv_s = ([] for _ in range(10))
    for l in range(depth):
        shift_p, scale_p, gate_p = mod_parts(l, 0, bp)
        shift_s, scale_s, gate_s = mod_parts(l, bp, bp + bs_)
        g_l = norm_g[l].reshape(1, 1, d)
        last = l == depth - 1
        fg = final_g.reshape(1, 1, d) if last else None
        if l % 2 == 0:
            e = l // 2
            w = w_in_e[e]
            c0 = 2 * nqa + W_A
            c1 = c0 + GLA_LOWRANK
            w_cat = jnp.concatenate([w[:, :c0], w[:, c0:c1], jnp.zeros((d, LANE - GLA_LOWRANK), F32), w[:, c1:]],
                                    axis=1).astype(BF16)
            o0 = c0 + LANE
            splits = [(0, c0, 1.0, 0), (c0, LANE, 1.0, 0), (o0, 2 * H_B * DK_B, DK_B ** -0.5, 0),
                      (o0 + 512, 512, 1.0, H_B), (o0 + 1024, 512, 1.0, H_B), (o0 + 1536, W_A + W_B, 1.0, 0)]
            splits_s = [sp[:3] + (0,) for sp in splits]
            dtypes = [F32, F32, BF16, F32, F32, F32]
            wg = jnp.concatenate([w_gla_gate[e], jnp.zeros((LANE - GLA_LOWRANK, nqa), F32)], axis=0).astype(BF16)
            bg = b_gla_gate[e].reshape(1, nqa)
            gn_a = g_gla_norm[e].reshape(1, DV_A)
            gn_b = g_diff_norm[e].reshape(1, DV_B)
            lvec = jnp.stack([lam_q1[e], lam_k1[e], lam_q2[e], lam_k2[e]])
            lam_init = 0.8 - 0.6 * math.exp(-0.3 * l)
            w_out = w_out_e[e].astype(BF16)

            qkv_a, glow, q_b, k_b, v_b, gate, k_hd, v_hd = _inproj(yp, shift_p, scale_p, g_l, w_cat, splits, dtypes,
                                                                   1, tt_p)
            o_a, s_new = _gla_prompt(qkv_a, glow, wg, bg, gn_a, tt_p)
            o_b = _diff_prompt(q_b, k_b, v_b, slopes_b, lvec, gn_b, lam_init, tq_diff)
            yp = _outproj(yp, [o_a, o_b], gate, gate_p, w_out, fg, 1, tt_p)
            gla_p.append(s_new)
            dk_p.append(k_hd.reshape(bp, tp, H_B, 2 * DK_B))
            dv_p.append(v_hd.reshape(bp, tp, H_B, DV_B))

            dtypes_s = [F32] * len(dtypes)
            qkv_a, glow, q_b, k_b, v_b, gate = _inproj(ys, shift_s, scale_s, g_l, w_cat, splits_s, dtypes_s, nb_s, ts)
            o_a, s_new = _gla_sample(qkv_a, glow, state_gla[e], wg, bg, gn_a, min(8, bs_))
            ck = cache_diff_k.reshape(-1, page * H_B, 2 * DK_B)
            cv = cache_diff_v.reshape(-1, page * H_B, DV_B)
            o_b = _diff_sample(q_b, k_b, v_b, ck, cv, page_table, e * n_pool, lvec, gn_b, lam_init)
            ys = _outproj(ys, [o_a, o_b], gate, gate_s, w_out, fg, nb_s, ts)
            gla_s.append(s_new)
            dk_s.append(k_b.reshape(bs_, ts, H_B, 2 * DK_B))
            dv_s.append(v_b.reshape(bs_, ts, H_B, DV_B))
        else:
            o = l // 2
            w_cat = w_in_o[o].astype(BF16)
            splits = [(0, W_C, 1.0, 0), (W_C, W_C, 1.0, H_C), (2 * W_C, W_C, 1.0, H_C), (3 * W_C, W_C, 1.0, 0)]
            splits_s = [sp[:3] + (0,) for sp in splits]
            dtypes = [F32, F32, F32, F32]
            w_out = w_out_o[o].astype(BF16)

            q, k, v, gate, k_hd, v_hd = _inproj(yp, shift_p, scale_p, g_l, w_cat, splits, dtypes, 1, tt_p)
            att = _moba_prompt(q, k, v, slopes_c)
            yp = _outproj(yp, [att], gate, gate_p, w_out, fg, 1, tt_p)
            mk_p.append(k_hd.reshape(bp, tp, H_C, DH_C))
            mv_p.append(v_hd.reshape(bp, tp, H_C, DH_C))

            q, k, v, gate = _inproj(ys, shift_s, scale_s, g_l, w_cat, splits_s, dtypes, nb_s, ts)
            ck = cache_moba_k.reshape(-1, page * H_C, DH_C)
            cv = cache_moba_v.reshape(-1, page * H_C, DH_C)
            att = _moba_sample(q, k, v, ck, cv, page_table, o * n_pool)
            ys = _outproj(ys, [att], gate, gate_s, w_out, fg, nb_s, ts)
            mk_s.append(k.reshape(bs_, ts, H_C, DH_C))
            mv_s.append(v.reshape(bs_, ts, H_C, DH_C))
    return (yp, ys, jnp.stack(gla_p), jnp.stack(gla_s), jnp.stack(dk_p), jnp.stack(dv_p), jnp.stack(dk_s),
            jnp.stack(dv_s), jnp.stack(mk_p), jnp.stack(mv_p), jnp.stack(mk_s), jnp.stack(mv_s))
```

```python
import functools
import math

import jax
import jax.numpy as jnp
import numpy as np
from jax import lax
from jax.experimental import pallas as pl
from jax.experimental.pallas import tpu as pltpu

F32 = jnp.float32
BF16 = jnp.bfloat16

D_MODEL = 1024
H_A, DK_A, DV_A = 4, 64, 128
GLA_LOWRANK = 16
GLA_TAU = 16.0
GLA_CHUNK = 64
GLA_ATT_ROWS = 256
H_B, DK_B, DV_B = 4, 64, 128
H_C, DH_C = 8, 128
MOBA_BLOCK = 256
MOBA_TOPK = 3
W_A = H_A * DV_A
W_B = H_B * DV_B
W_C = H_C * DH_C
EPS = 1e-6
NEG = -1e30
LANE = 128
VT_ROWS = LANE + 16
KV_GROUP_KEYS = 1024
MOBA_Q_TILE = 512
VMEM_LIMIT = 56 * 1024 * 1024

HIGHEST = lax.Precision.HIGHEST


def _dot(a, b):
    return jnp.dot(a, b, preferred_element_type=F32)


def _dot_nt(a, b, precision=None):
    return lax.dot_general(a, b, (((1,), (1,)), ((), ())), preferred_element_type=F32, precision=precision)


def _dot_tn(a, b):
    return lax.dot_general(a, b, (((0,), (0,)), ((), ())), preferred_element_type=F32)


def _rms(x, g):
    return x * lax.rsqrt(jnp.mean(x * x, axis=-1, keepdims=True) + EPS) * g


def _silu(x):
    return x * jax.nn.sigmoid(x)


def _params(*sem):
    return pltpu.CompilerParams(dimension_semantics=sem, vmem_limit_bytes=VMEM_LIMIT)


def _ada_kernel(c_ref, w_ref, b_ref, o_ref):
    a = _silu(c_ref[...]).astype(BF16)
    o_ref[0] = _dot(a, w_ref[0].astype(BF16)) + b_ref[0]


def _ada(c_all, w_ada, b_ada):
    depth = w_ada.shape[0]
    mp = c_all.shape[0]
    tn = 1024
    return pl.pallas_call(
        _ada_kernel,
        out_shape=jax.ShapeDtypeStruct((depth, mp, 3 * D_MODEL), F32),
        grid=(depth, 3 * D_MODEL // tn),
        in_specs=[
            pl.BlockSpec((mp, D_MODEL), lambda l, j: (0, 0)),
            pl.BlockSpec((1, D_MODEL, tn), lambda l, j: (l, 0, j)),
            pl.BlockSpec((1, 1, tn), lambda l, j: (l, 0, j)),
        ],
        out_specs=pl.BlockSpec((1, mp, tn), lambda l, j: (l, 0, j)),
        compiler_params=_params("arbitrary", "arbitrary"),
        name="ada_mod",
    )(c_all, w_ada, b_ada.reshape(depth, 1, 3 * D_MODEL))


def _inproj_kernel(x_ref, shift_ref, scale_ref, g_ref, w_ref, *out_refs, splits):
    nb, tt, d = x_ref.shape
    x = x_ref[...]
    h = _rms(x, g_ref[...]) * (1.0 + scale_ref[...]) + shift_ref[...]
    h2 = h.reshape(nb * tt, d).astype(BF16)
    head_refs = list(out_refs[len(splits):])
    for (start, width, mult, n_heads), o_ref in zip(splits, out_refs):
        z = _dot(h2, w_ref[:, start:start + width])
        if mult != 1.0:
            z = z * mult
        o_ref[...] = z.reshape(nb, tt, width).astype(o_ref.dtype)
        if n_heads:
            hr = head_refs.pop(0)
            for hd in range(n_heads):
                hr.at[0][pl.ds(hd, tt, stride=n_heads), :] = z[:, hd * LANE:(hd + 1) * LANE]


def _inproj(x, shift, scale, g, w, splits, dtypes, nb, tt):
    nseq, t, d = x.shape
    n = w.shape[1]
    grid = (nseq // nb, t // tt)
    xmap = lambda i, j: (i, j, 0)
    out_shape = [jax.ShapeDtypeStruct((nseq, t, wd), dt) for (_, wd, _, _), dt in zip(splits, dtypes)]
    out_specs = [pl.BlockSpec((nb, tt, wd), xmap) for (_, wd, _, _) in splits]
    for (_, wd, _, nh) in splits:
        if nh:
            assert nb == 1 and wd == nh * LANE
            out_shape.append(jax.ShapeDtypeStruct((nseq, t * nh, LANE), F32))
            out_specs.append(pl.BlockSpec((1, tt * nh, LANE), xmap))
    return pl.pallas_call(
        functools.partial(_inproj_kernel, splits=tuple(splits)),
        out_shape=out_shape,
        grid=grid,
        in_specs=[
            pl.BlockSpec((nb, tt, d), xmap),
            pl.BlockSpec((nb, 1, d), lambda i, j: (i, 0, 0)),
            pl.BlockSpec((nb, 1, d), lambda i, j: (i, 0, 0)),
            pl.BlockSpec((1, 1, d), lambda i, j: (0, 0, 0)),
            pl.BlockSpec((d, n), lambda i, j: (0, 0)),
        ],
        out_specs=out_specs,
        compiler_params=_params("arbitrary", "arbitrary"),
        name="norm_inproj",
    )(x, shift, scale, g, w)


def _outproj_kernel(*refs, n_parts, final):
    x_ref = refs[0]
    o_refs = refs[1:1 + n_parts]
    gate_ref, ag_ref, w_ref = refs[1 + n_parts:4 + n_parts]
    rest = refs[4 + n_parts:]
    fg_ref = rest[0] if final else None
    y_ref = rest[-1]
    nb, tt, d = x_ref.shape
    rows = nb * tt
    sg = _silu(gate_ref[...].reshape(rows, -1))
    acc = jnp.zeros((rows, d), F32)
    col = 0
    for o_ref in o_refs:
        wd = o_ref.shape[-1]
        u = (o_ref[...].reshape(rows, wd) * sg[:, col:col + wd]).astype(BF16)
        acc = acc + _dot(u, w_ref[col:col + wd, :])
        col += wd
    y = x_ref[...] + ag_ref[...] * acc.reshape(nb, tt, d)
    if final:
        y = _rms(y, fg_ref[...])
    y_ref[...] = y


def _outproj(x, o_parts, gate, ada_gate, w, final_g, nb, tt):
    nseq, t, d = x.shape
    grid = (nseq // nb, t // tt)
    xmap = lambda i, j: (i, j, 0)
    final = final_g is not None
    in_specs = [pl.BlockSpec((nb, tt, d), xmap)]
    in_specs += [pl.BlockSpec((nb, tt, o.shape[-1]), xmap) for o in o_parts]
    in_specs += [
        pl.BlockSpec((nb, tt, gate.shape[-1]), xmap),
        pl.BlockSpec((nb, 1, d), lambda i, j: (i, 0, 0)),
        pl.BlockSpec(w.shape, lambda i, j: (0, 0)),
    ]
    args = [x, *o_parts, gate, ada_gate, w]
    if final:
        in_specs.append(pl.BlockSpec((1, 1, d), lambda i, j: (0, 0, 0)))
        args.append(final_g)
    return pl.pallas_call(
        functools.partial(_outproj_kernel, n_parts=len(o_parts), final=final),
        out_shape=jax.ShapeDtypeStruct((nseq, t, d), F32),
        grid=grid,
        in_specs=in_specs,
        out_specs=pl.BlockSpec((nb, tt, d), xmap),
        compiler_params=_params("arbitrary", "arbitrary"),
        name="gate_outproj",
    )(*args)


def _gla_rows(q, k, v, glow, wg, bg, gnorm, states, seq_len):
    r = q.shape[0]
    ns = r // seq_len
    la = jax.nn.log_sigmoid(_dot(glow.astype(BF16), wg) + bg) / GLA_TAU
    ri = lax.broadcasted_iota(jnp.int32, (r, r), 0)
    ci = lax.broadcasted_iota(jnp.int32, (r, r), 1)
    same = (ri // seq_len) == (ci // seq_len)
    causal = same & (ci <= ri)
    cum = jnp.dot(causal.astype(F32), la, preferred_element_type=F32, precision=HIGHEST)
    tot = jnp.dot(same.astype(F32), la, preferred_element_type=F32, precision=HIGHEST)
    q_all = q * (DK_A ** -0.5) * jnp.exp(cum)
    k_all = k * jnp.exp(-cum)
    ke_all = k * jnp.exp(tot - cum)
    dec = jnp.exp(tot)
    er = lax.broadcasted_iota(jnp.int32, (DK_A, DK_A), 0)
    ec = lax.broadcasted_iota(jnp.int32, (DK_A, DK_A), 1)
    eye = (er == ec).astype(F32)
    outs = []
    new_states = [[None] * H_A for _ in range(ns)]
    for h in range(H_A):
        ks = slice(h * DK_A, (h + 1) * DK_A)
        vs = slice(h * DV_A, (h + 1) * DV_A)
        q_in = q_all[:, ks]
        k_end = ke_all[:, ks]
        v_h = v[:, vs]
        att = jnp.where(causal, _dot_nt(q_in.astype(BF16), k_all[:, ks].astype(BF16)), 0.0)
        o = _dot(att.astype(BF16), v_h.astype(BF16))
        o_rows = []
        for j in range(ns):
            rs = slice(j * seq_len, (j + 1) * seq_len)
            s_old = states[j][h]
            o_rows.append(o[rs] + _dot(q_in[rs].astype(BF16), s_old.astype(BF16)))
            d_col = jnp.sum(eye * dec[j * seq_len:j * seq_len + 1, ks], axis=1, keepdims=True)
            new_states[j][h] = s_old * d_col + _dot_tn(k_end[rs].astype(BF16), v_h[rs].astype(BF16))
        o = o_rows[0] if ns == 1 else jnp.concatenate(o_rows, axis=0)
        outs.append(_rms(o, gnorm))
    return outs, new_states


def _gla_prompt_kernel(qkv_ref, glow_ref, wg_ref, bg_ref, gn_ref, o_ref, s_ref, st_sc):
    t = pl.program_id(1)
    tt = qkv_ref.shape[1]
    ch = GLA_CHUNK
    nc = tt // ch
    nq = H_A * DK_A

    @pl.when(t == 0)
    def _():
        st_sc[...] = jnp.zeros_like(st_sc)

    q = qkv_ref[0, :, 0:nq]
    k = qkv_ref[0, :, nq:2 * nq]
    la = jax.nn.log_sigmoid(_dot(glow_ref[0].astype(BF16), wg_ref[...]) + bg_ref[...]) / GLA_TAU
    la_w = jnp.concatenate([la[c * ch:(c + 1) * ch] for c in range(nc)], axis=1)
    ri = lax.broadcasted_iota(jnp.int32, (ch, ch), 0)
    ci = lax.broadcasted_iota(jnp.int32, (ch, ch), 1)
    cum_w = jnp.dot((ci <= ri).astype(F32), la_w, preferred_element_type=F32, precision=HIGHEST)
    tot_w = jnp.broadcast_to(cum_w[ch - 1:ch], cum_w.shape)
    tall = lambda a: jnp.concatenate([a[:, c * nq:(c + 1) * nq] for c in range(nc)], axis=0)
    cum = tall(cum_w)
    tot = tall(tot_w)
    q_all = q * (DK_A ** -0.5) * jnp.exp(cum)
    k_all = k * jnp.exp(-cum)
    ke_all = k * jnp.exp(tot - cum)
    dec = jnp.exp(tot)
    sub = min(GLA_ATT_ROWS, tt)
    rr = lax.broadcasted_iota(jnp.int32, (sub, sub), 0)
    cc = lax.broadcasted_iota(jnp.int32, (sub, sub), 1)
    causal = (rr // ch == cc // ch) & (cc <= rr)
    eye = (ri[:DK_A, :DK_A] == ci[:DK_A, :DK_A]).astype(F32)
    gn = gn_ref[...]
    for h in range(H_A):
        ks = slice(h * DK_A, (h + 1) * DK_A)
        q_in = q_all[:, ks].astype(BF16)
        k_end = ke_all[:, ks].astype(BF16)
        v_h = qkv_ref[0, :, 2 * nq + h * DV_A:2 * nq + (h + 1) * DV_A].astype(BF16)
        k_in = k_all[:, ks].astype(BF16)
        o_parts = []
        for s0 in range(0, tt, sub):
            ss = slice(s0, s0 + sub)
            att = jnp.where(causal, _dot_nt(q_in[ss], k_in[ss]), 0.0)
            o_parts.append(_dot(att.astype(BF16), v_h[ss]))
        o = jnp.concatenate(o_parts, axis=0)
        s_h = st_sc[h]
        rows_out = []
        for c in range(nc):
            rs = slice(c * ch, (c + 1) * ch)
            rows_out.append(o[rs] + _dot(q_in[rs], s_h.astype(BF16)))
            d_col = jnp.sum(eye * dec[c * ch:c * ch + 1, ks], axis=1, keepdims=True)
            s_h = s_h * d_col + _dot_tn(k_end[rs], v_h[rs])
        st_sc[h] = s_h
        o_ref[0, :, h * DV_A:(h + 1) * DV_A] = _rms(jnp.concatenate(rows_out, axis=0), gn)

    @pl.when(t == pl.num_programs(1) - 1)
    def _():
        s_ref[0] = st_sc[...]


def _gla_prompt(qkv, glow, wg, bg, gn, tt):
    b, t, _ = qkv.shape
    assert t % tt == 0 and tt % GLA_CHUNK == 0
    return pl.pallas_call(
        _gla_prompt_kernel,
        out_shape=[jax.ShapeDtypeStruct((b, t, W_A), F32), jax.ShapeDtypeStruct((b, H_A, DK_A, DV_A), F32)],
        grid=(b, t // tt),
        in_specs=[
            pl.BlockSpec((1, tt, qkv.shape[-1]), lambda i, j: (i, j, 0)),
            pl.BlockSpec((1, tt, LANE), lambda i, j: (i, j, 0)),
            pl.BlockSpec(wg.shape, lambda i, j: (0, 0)),
            pl.BlockSpec(bg.shape, lambda i, j: (0, 0)),
            pl.BlockSpec(gn.shape, lambda i, j: (0, 0)),
        ],
        out_specs=[
            pl.BlockSpec((1, tt, W_A), lambda i, j: (i, j, 0)),
            pl.BlockSpec((1, H_A, DK_A, DV_A), lambda i, j: (i, 0, 0, 0)),
        ],
        scratch_shapes=[pltpu.VMEM((H_A, DK_A, DV_A), F32)],
        compiler_params=_params("arbitrary", "arbitrary"),
        name="gla_prompt",
    )(qkv, glow, wg, bg, gn)


def _gla_sample_kernel(qkv_ref, glow_ref, s0_ref, wg_ref, bg_ref, gn_ref, o_ref, s_ref):
    nb, ts, _ = qkv_ref.shape
    nq = H_A * DK_A
    r = nb * ts
    qkv = qkv_ref[...].reshape(r, qkv_ref.shape[-1])
    glow = glow_ref[...].reshape(r, LANE)
    states = [[s0_ref[j, h] for h in range(H_A)] for j in range(nb)]
    outs, new_states = _gla_rows(qkv[:, 0:nq], qkv[:, nq:2 * nq], qkv[:, 2 * nq:2 * nq + W_A], glow,
                                 wg_ref[...], bg_ref[...], gn_ref[...], states, ts)
    for h in range(H_A):
        o_ref[:, :, h * DV_A:(h + 1) * DV_A] = outs[h].reshape(nb, ts, DV_A)
        for j in range(nb):
            s_ref[j, h] = new_states[j][h]


def _gla_sample(qkv, glow, s0, wg, bg, gn, nb):
    b, ts, _ = qkv.shape
    assert b % nb == 0 and ts % 8 == 0 and ts <= GLA_CHUNK
    return pl.pallas_call(
        _gla_sample_kernel,
        out_shape=[jax.ShapeDtypeStruct((b, ts, W_A), F32), jax.ShapeDtypeStruct((b, H_A, DK_A, DV_A), F32)],
        grid=(b // nb,),
        in_specs=[
            pl.BlockSpec((nb, ts, qkv.shape[-1]), lambda i: (i, 0, 0)),
            pl.BlockSpec((nb, ts, LANE), lambda i: (i, 0, 0)),
            pl.BlockSpec((nb, H_A, DK_A, DV_A), lambda i: (i, 0, 0, 0)),
            pl.BlockSpec(wg.shape, lambda i: (0, 0)),
            pl.BlockSpec(bg.shape, lambda i: (0, 0)),
            pl.BlockSpec(gn.shape, lambda i: (0, 0)),
        ],
        out_specs=[
            pl.BlockSpec((nb, ts, W_A), lambda i: (i, 0, 0)),
            pl.BlockSpec((nb, H_A, DK_A, DV_A), lambda i: (i, 0, 0, 0)),
        ],
        compiler_params=_params("arbitrary"),
        name="gla_sample",
    )(qkv, glow, s0, wg, bg, gn)


def _attend_cols(kaug_sc, vt_sc, q_aug, i, tq, acc_sc):
    ncol = q_aug.shape[0]
    t = kaug_sc.shape[0]
    gk = min(KV_GROUP_KEYS, t)
    assert gk % tq == 0 and t % gk == 0
    kr = lax.broadcasted_iota(jnp.int32, (gk, ncol), 0)
    qc = lax.broadcasted_iota(jnp.int32, (gk, ncol), 1) % tq
    for v in range(t // gk):
        head = v * gk

        @pl.when((i * tq) // gk == v)
        def _(head=head):
            s_t = _dot_nt(kaug_sc[head:head + gk, :], q_aug)
            s_t = jnp.where(kr + head <= qc + i * tq, s_t, NEG)
            m = jnp.max(s_t, axis=0, keepdims=True)
            if head > 0:
                s_h = _dot_nt(kaug_sc[0:head, :], q_aug)
                m = jnp.maximum(m, jnp.max(s_h, axis=0, keepdims=True))
            acc = _dot(vt_sc[:, head:head + gk], jnp.exp(s_t - m).astype(BF16))
            if head > 0:
                acc = acc + _dot(vt_sc[:, 0:head], jnp.exp(s_h - m).astype(BF16))
            acc_sc[...] = acc


def _fill_kv(k_ref, v_ref, kaug_sc, vt_sc, tk, aug_fn, km_sc=None):
    nblk = kaug_sc.shape[0] // tk
    dv = v_ref.shape[-1]
    rowv = lax.broadcasted_iota(jnp.int32, (VT_ROWS - dv, tk), 0)
    ones_rows = jnp.where(rowv == 0, 1.0, 0.0).astype(BF16)
    for n in range(nblk):
        rows = slice(n * tk, (n + 1) * tk)
        kt = k_ref[0, rows, :]
        if km_sc is not None:
            km_sc[n:n + 1, :] = jnp.mean(kt, axis=0, keepdims=True)
        kaug_sc[rows, 0:LANE] = kt.astype(BF16)
        kaug_sc[rows, LANE:] = aug_fn(n).astype(BF16)
        vt_sc[0:dv, rows] = v_ref[0, rows, :].T.astype(BF16)
        vt_sc[dv:, rows] = ones_rows


def _lambda(l_ref, lam_init):
    lv = l_ref[...]
    s1 = jnp.sum(lv[0:1] * lv[1:2], axis=1, keepdims=True)
    s2 = jnp.sum(lv[2:3] * lv[3:4], axis=1, keepdims=True)
    return jnp.exp(s1) - jnp.exp(s2) + lam_init


def _diff_prompt_kernel(slope_ref, q_ref, k_ref, v_ref, l_ref, gn_ref, o_ref, kaug_sc, vt_sc, acc_sc, *, lam_init):
    h = pl.program_id(1)
    i = pl.program_id(2)
    tq = q_ref.shape[1]

    @pl.when(i == 0)
    def _():
        slope = slope_ref[h]
        c = lax.broadcasted_iota(jnp.int32, (tq, LANE), 0).astype(F32)
        col = lax.broadcasted_iota(jnp.int32, (tq, LANE), 1)

        def aug(n):
            return jnp.where(col == 0, slope * float(tq * n), jnp.where(col == 1, slope * c, 0.0))

        _fill_kv(k_ref, v_ref, kaug_sc, vt_sc, tq, aug)

    q = q_ref[0]
    lane = lax.broadcasted_iota(jnp.int32, q.shape, 1)
    zero = jnp.zeros_like(q)
    ones2 = jnp.where(lane < 2, 1.0, 0.0).astype(BF16)
    q_aug = jnp.concatenate([
        jnp.concatenate([jnp.where(lane < DK_B, q, zero), ones2], axis=1),
        jnp.concatenate([jnp.where(lane >= DK_B, q, zero), ones2], axis=1)], axis=0)
    _attend_cols(kaug_sc, vt_sc, q_aug, i, tq, acc_sc)
    acc = acc_sc[...]
    o_t = acc[0:DV_B] / acc[DV_B:DV_B + 1]
    lam = _lambda(l_ref, lam_init)
    o = (o_t[:, :tq] - lam * o_t[:, tq:]).T
    o_ref[0] = _rms(o, gn_ref[...]) * (1.0 - lam_init)


def _diff_prompt(q, k, v, slopes, lvec, gn, lam_init, tq):
    b, t, _ = q.shape
    assert t % tq == 0
    nblk = t // tq
    kv_spec = pl.BlockSpec((1, t, LANE), lambda bi, h, i, s: (bi, 0, h))
    return pl.pallas_call(
        functools.partial(_diff_prompt_kernel, lam_init=lam_init),
        out_shape=jax.ShapeDtypeStruct((b, t, W_B), F32),
        grid_spec=pltpu.PrefetchScalarGridSpec(
            num_scalar_prefetch=1,
            grid=(b, H_B, nblk),
            in_specs=[
                pl.BlockSpec((1, tq, LANE), lambda bi, h, i, s: (bi, i, h)),
                kv_spec,
                kv_spec,
                pl.BlockSpec(lvec.shape, lambda bi, h, i, s: (0, 0)),
                pl.BlockSpec(gn.shape, lambda bi, h, i, s: (0, 0)),
            ],
            out_specs=pl.BlockSpec((1, tq, LANE), lambda bi, h, i, s: (bi, i, h)),
            scratch_shapes=[
                pltpu.VMEM((t, 2 * LANE), BF16),
                pltpu.VMEM((VT_ROWS, t), BF16),
                pltpu.VMEM((VT_ROWS, 2 * tq), F32),
            ],
        ),
        compiler_params=_params("arbitrary", "arbitrary", "arbitrary"),
        name="diff_prompt",
    )(slopes, q, k, v, lvec, gn)


def _topk_mask(g, valid, n_axis):
    nb = g.shape[n_axis]
    g = jnp.where(valid, g, -jnp.inf)
    idx = lax.broadcasted_iota(jnp.int32, g.shape, n_axis)
    rank = jnp.zeros(g.shape, jnp.int32)
    for m in range(nb):
        gm = lax.slice_in_dim(g, m, m + 1, axis=n_axis)
        beats = (gm > g) | ((gm == g) & (m < idx))
        rank = rank + beats.astype(jnp.int32)
    return (rank < MOBA_TOPK) & valid


def _moba_prompt_kernel(slope_ref, q_ref, k_ref, v_ref, o_ref, km_sc, kaug_sc, vt_sc, acc_sc):
    h = pl.program_id(1)
    i = pl.program_id(2)
    tq = q_ref.shape[1]
    bs = MOBA_BLOCK
    nblk = km_sc.shape[0]

    @pl.when(i == 0)
    def _():
        slope = slope_ref[h]
        c = lax.broadcasted_iota(jnp.int32, (bs, LANE), 0).astype(F32)
        col = lax.broadcasted_iota(jnp.int32, (bs, LANE), 1)

        def aug(n):
            return jnp.where(col == n, 1.0, jnp.where(col == nblk, slope * float(bs * n),
                                                      jnp.where(col == nblk + 1, slope * c, 0.0)))

        _fill_kv(k_ref, v_ref, kaug_sc, vt_sc, bs, aug, km_sc)

    q = q_ref[0]
    g = _dot_nt(km_sc[...], q, precision=HIGHEST)
    blk = lax.broadcasted_iota(jnp.int32, g.shape, 0)
    own = i * (tq // bs) + lax.broadcasted_iota(jnp.int32, g.shape, 1) // bs
    sel = _topk_mask(g, blk < own, 0)
    selb = jnp.where(sel | (blk == own), 0.0, NEG)
    row = lax.broadcasted_iota(jnp.int32, (LANE, tq), 0)
    coef = jnp.concatenate([selb, jnp.zeros((LANE - nblk, tq), F32)], axis=0)
    coef = jnp.where((row == nblk) | (row == nblk + 1), 1.0, coef)
    q_aug = jnp.concatenate([q * (DH_C ** -0.5), coef.T], axis=1).astype(BF16)
    _attend_cols(kaug_sc, vt_sc, q_aug, i, tq, acc_sc)
    acc = acc_sc[...]
    o_ref[0] = (acc[0:DH_C] / acc[DH_C:DH_C + 1]).T


def _moba_prompt(q, k, v, slopes):
    b, t, _ = q.shape
    bs = MOBA_BLOCK
    tq = min(MOBA_Q_TILE, t)
    assert t % tq == 0 and tq % bs == 0
    nblk = t // bs
    assert nblk + 2 <= LANE
    kv_spec = pl.BlockSpec((1, t, LANE), lambda bi, h, i, s: (bi, 0, h))
    q_spec = pl.BlockSpec((1, tq, LANE), lambda bi, h, i, s: (bi, i, h))
    return pl.pallas_call(
        _moba_prompt_kernel,
        out_shape=jax.ShapeDtypeStruct((b, t, W_C), F32),
        grid_spec=pltpu.PrefetchScalarGridSpec(
            num_scalar_prefetch=1,
            grid=(b, H_C, t // tq),
            in_specs=[q_spec, kv_spec, kv_spec],
            out_specs=q_spec,
            scratch_shapes=[
                pltpu.VMEM((nblk, DH_C), F32),
                pltpu.VMEM((t, 2 * LANE), BF16),
                pltpu.VMEM((VT_ROWS, t), BF16),
                pltpu.VMEM((VT_ROWS, tq), F32),
            ],
        ),
        compiler_params=_params("arbitrary", "arbitrary", "arbitrary"),
        name="moba_prompt",
    )(slopes, q, k, v)


def _score_cols_mask(n_col, n_heads, n_maps, dk, ts):
    c = np.arange(n_col)[:, None]
    f = np.arange(n_heads * n_maps * dk)[None, :]
    return (((c % LANE) // ts == f // (n_maps * dk)) & (c // LANE == (f % (n_maps * dk)) // dk)).astype(np.float32)


def _score_bias(n_col, n_heads, ts, past_len, lpad):
    j = np.arange(lpad)[:, None]
    c = np.arange(n_col)[None, :]
    hh = (c % LANE) // ts
    rel = past_len + c % ts - j
    slope = np.array([2.0 ** (-8.0 * (h + 1) / n_heads) for h in range(LANE // ts + 1)], np.float32)[hh]
    bias = np.where(hh < n_heads, -slope * rel.astype(np.float32), np.float32(0.0))
    return np.where((rel >= 0) & (j < past_len + ts), bias, np.float32(NEG)).astype(np.float32)


def _head_pair(p_ref, j, n_heads, page):
    rows = p_ref.at[0]
    return jnp.concatenate([rows[pl.ds(2 * j, page, stride=n_heads), :],
                            rows[pl.ds(2 * j + 1, page, stride=n_heads), :]], axis=1).astype(BF16)


def _page_scores(q, wmask_ref, kp_refs, kn_ref, s_sc, past_len, n_heads):
    ts, width = q.shape
    ncol = s_sc.shape[1]
    page = kp_refs[0].shape[1] // n_heads
    wt = (jnp.broadcast_to(q[None], (ncol // ts, ts, width)).reshape(ncol, width) * wmask_ref[...]).astype(BF16)
    for g, kp_ref in enumerate(kp_refs):
        acc = None
        for j in range(n_heads // 2):
            part = _dot_nt(_head_pair(kp_ref, j, n_heads, page), wt[:, 2 * j * LANE:(2 * j + 2) * LANE])
            acc = part if acc is None else acc + part
        s_sc[g * page:(g + 1) * page, :] = acc
    s_sc[past_len:past_len + ts, :] = _dot_nt(kn_ref[0].astype(BF16), wt)
    s_sc[past_len + ts:, :] = jnp.zeros((s_sc.shape[0] - past_len - ts, ncol), F32)


def _page_values(a_t, vp_refs, vn_ref, past_len, n_heads):
    page = vp_refs[0].shape[1] // n_heads
    ts, width = vn_ref.shape[1], vn_ref.shape[2]
    acc = [jnp.zeros((2 * ts, 2 * LANE), F32) for _ in range(n_heads // 2)]
    for g, vp_ref in enumerate(vp_refs):
        at = a_t[g * page:(g + 1) * page].T
        for j in range(n_heads // 2):
            acc[j] = acc[j] + _dot(at[2 * ts * j:2 * ts * (j + 1)].astype(BF16), _head_pair(vp_ref, j, n_heads, page))
    v_tail = jnp.concatenate([vn_ref[0], jnp.zeros((LANE - ts, width), F32)], axis=0)
    o_tail = _dot(a_t[past_len:].T.astype(BF16), v_tail.astype(BF16))
    outs = []
    for h in range(n_heads):
        j, r = divmod(h, 2)
        outs.append(acc[j][r * ts:(r + 1) * ts, r * LANE:(r + 1) * LANE]
                    + o_tail[h * ts:(h + 1) * ts, h * LANE:(h + 1) * LANE])
    return outs


def _softmax_rows(s):
    e = jnp.exp(s - jnp.max(s, axis=0, keepdims=True))
    return e * (1.0 / jnp.sum(e, axis=0, keepdims=True))


def _diff_sample_kernel(pt_ref, q_ref, kn_ref, vn_ref, wmask_ref, bias_ref, l_ref, gn_ref, *rest,
                        n_pages, lam_init, past_len):
    kp_refs, vp_refs = rest[:n_pages], rest[n_pages:2 * n_pages]
    o_ref, s_sc = rest[2 * n_pages:]
    ts = q_ref.shape[1]
    _page_scores(q_ref[0], wmask_ref, kp_refs, kn_ref, s_sc, past_len, H_B)
    pr = _softmax_rows(s_sc[...] + bias_ref[...])
    lam = _lambda(l_ref, lam_init)
    a_t = pr[:, :LANE] - lam * pr[:, LANE:]
    gn = gn_ref[...]
    for h, o in enumerate(_page_values(a_t, vp_refs, vn_ref, past_len, H_B)):
        o_ref[0, :, h * DV_B:(h + 1) * DV_B] = _rms(o, gn) * (1.0 - lam_init)


def _moba_sample_kernel(pt_ref, q_ref, kn_ref, vn_ref, wmask_ref, bias_ref, *rest, n_pages, past_len):
    kp_refs, vp_refs = rest[:n_pages], rest[n_pages:2 * n_pages]
    o_ref, s_sc = rest[2 * n_pages:]
    ts = q_ref.shape[1]
    bs = MOBA_BLOCK
    nbp = past_len // bs
    _page_scores(q_ref[0] * (DH_C ** -0.5), wmask_ref, kp_refs, kn_ref, s_sc, past_len, H_C)
    s = s_sc[...]
    blocks = [s[n * bs:(n + 1) * bs] for n in range(nbp)]
    g = jnp.concatenate([jnp.sum(b, axis=0, keepdims=True) for b in blocks], axis=0)
    selb = jnp.where(_topk_mask(g, jnp.full(g.shape, True), 0), 0.0, NEG)
    bias = bias_ref[...]
    parts = [blocks[n] + bias[n * bs:(n + 1) * bs] + selb[n:n + 1] for n in range(nbp)]
    parts.append(s[past_len:] + bias[past_len:])
    pr = _softmax_rows(jnp.concatenate(parts, axis=0))
    for h, o in enumerate(_page_values(pr, vp_refs, vn_ref, past_len, H_C)):
        o_ref[0, :, h * DH_C:(h + 1) * DH_C] = o


def _paged_call(body, name, q, k_new, v_new, cache_k, cache_v, page_table, page_base, n_heads, n_maps, extra):
    b, ts, width = q.shape
    n_pages = page_table.shape[1]
    page = cache_k.shape[1] // n_heads
    vwidth = v_new.shape[2]
    past_len = n_pages * page
    assert n_heads * ts <= LANE and ts % 8 == 0 and n_heads % 2 == 0
    assert width == n_heads * LANE and vwidth == n_heads * LANE and cache_k.shape[2] == LANE
    ncol = n_maps * LANE
    lpad = past_len + LANE
    wmask = jnp.asarray(_score_cols_mask(ncol, n_heads, n_maps, width // (n_heads * n_maps), ts))
    bias = jnp.asarray(_score_bias(ncol, n_heads, ts, past_len, lpad))
    const = lambda a: pl.BlockSpec(a.shape, lambda bi, pt: (0,) * a.ndim)
    row = lambda w: pl.BlockSpec((1, ts, w), lambda bi, pt: (bi, 0, 0))
    page_spec = lambda g: pl.BlockSpec((1, page * n_heads, LANE),
                                       lambda bi, pt: (page_base + pt[bi * n_pages + g], 0, 0))
    in_specs = [row(width), row(width), row(vwidth), const(wmask), const(bias)] + [const(a) for a in extra]
    in_specs += [page_spec(g) for g in range(n_pages)] * 2
    return pl.pallas_call(
        functools.partial(body, n_pages=n_pages, past_len=past_len),
        out_shape=jax.ShapeDtypeStruct((b, ts, vwidth), F32),
        grid_spec=pltpu.PrefetchScalarGridSpec(
            num_scalar_prefetch=1,
            grid=(b,),
            in_specs=in_specs,
            out_specs=row(vwidth),
            scratch_shapes=[pltpu.VMEM((lpad, ncol), F32)],
        ),
        compiler_params=_params("arbitrary"),
        name=name,
    )(page_table.reshape(-1), q, k_new, v_new, wmask, bias, *extra, *([cache_k] * n_pages), *([cache_v] * n_pages))


def _diff_sample(q, k_new, v_new, cache_k, cache_v, page_table, page_base, lvec, gn, lam_init):
    body = functools.partial(_diff_sample_kernel, lam_init=lam_init)
    return _paged_call(body, "diff_sample", q, k_new, v_new, cache_k, cache_v, page_table, page_base, H_B, 2,
                       [lvec, gn])


def _moba_sample(q, k_new, v_new, cache_k, cache_v, page_table, page_base):
    past_len = page_table.shape[1] * cache_k.shape[1] // H_C
    assert past_len % MOBA_BLOCK == 0 and past_len // MOBA_BLOCK >= MOBA_TOPK and q.shape[1] <= MOBA_BLOCK
    return _paged_call(_moba_sample_kernel, "moba_sample", q, k_new, v_new, cache_k, cache_v, page_table, page_base,
                       H_C, 1, [])


def _alibi_slopes(n):
    return jnp.array([2.0 ** (-8.0 * (h + 1) / n) for h in range(n)], dtype=F32)


def kernel(x_prompt, x_sample, c_prompt, c_sample, state_gla, cache_diff_k, cache_diff_v, cache_moba_k, cache_moba_v,
           page_table, norm_g, w_ada, b_ada, w_in_e, w_gla_gate, b_gla_gate, g_gla_norm, lam_q1, lam_k1, lam_q2,
           lam_k2, g_diff_norm, w_out_e, w_in_o, w_out_o, final_g):
    bp, tp, d = x_prompt.shape
    bs_, ts, _ = x_sample.shape
    depth = norm_g.shape[0]
    n_pool, page = cache_diff_k.shape[1], cache_diff_k.shape[2]
    assert d == D_MODEL

    tt_p = min(512, tp)
    nb_s = min(64, bs_)
    tq_diff = min(256, tp)

    mrows = bp + bs_
    mpad = -(-mrows // 8) * 8
    c_all = jnp.concatenate([c_prompt, c_sample, jnp.zeros((mpad - mrows, d), F32)], axis=0)
    mod = _ada(c_all, w_ada, b_ada)

    def mod_parts(l, lo, hi):
        m = mod[l, lo:hi]
        return m[:, None, 0:d], m[:, None, d:2 * d], m[:, None, 2 * d:3 * d]

    slopes_b = _alibi_slopes(H_B)
    slopes_c = _alibi_slopes(H_C)
    nqa = H_A * DK_A
    yp, ys = x_prompt, x_sample
    gla_p, gla_s, dk_p, dv_p, dk_s, dv_s, mk_p, mv_p, mk_s, mv_s = ([] for _ in range(10))
    for l in range(depth):
        shift_p, scale_p, gate_p = mod_parts(l, 0, bp)
        shift_s, scale_s, gate_s = mod_parts(l, bp, bp + bs_)
        g_l = norm_g[l].reshape(1, 1, d)
        last = l == depth - 1
        fg = final_g.reshape(1, 1, d) if last else None
        if l % 2 == 0:
            e = l // 2
            w = w_in_e[e]
            c0 = 2 * nqa + W_A
            c1 = c0 + GLA_LOWRANK
            w_cat = jnp.concatenate([w[:, :c0], w[:, c0:c1], jnp.zeros((d, LANE - GLA_LOWRANK), F32), w[:, c1:]],
                                    axis=1).astype(BF16)
            o0 = c0 + LANE
            splits = [(0, c0, 1.0, 0), (c0, LANE, 1.0, 0), (o0, 2 * H_B * DK_B, DK_B ** -0.5, 0),
                      (o0 + 512, 512, 1.0, H_B), (o0 + 1024, 512, 1.0, H_B), (o0 + 1536, W_A + W_B, 1.0, 0)]
            splits_s = [sp[:3] + (0,) for sp in splits]
            dtypes = [F32, F32, BF16, F32, F32, F32]
            wg = jnp.concatenate([w_gla_gate[e], jnp.zeros((LANE - GLA_LOWRANK, nqa), F32)], axis=0).astype(BF16)
            bg = b_gla_gate[e].reshape(1, nqa)
            gn_a = g_gla_norm[e].reshape(1, DV_A)
            gn_b = g_diff_norm[e].reshape(1, DV_B)
            lvec = jnp.stack([lam_q1[e], lam_k1[e], lam_q2[e], lam_k2[e]])
            lam_init = 0.8 - 0.6 * math.exp(-0.3 * l)
            w_out = w_out_e[e].astype(BF16)

            qkv_a, glow, q_b, k_b, v_b, gate, k_hd, v_hd = _inproj(yp, shift_p, scale_p, g_l, w_cat, splits, dtypes,
                                                                   1, tt_p)
            o_a, s_new = _gla_prompt(qkv_a, glow, wg, bg, gn_a, tt_p)
            o_b = _diff_prompt(q_b, k_b, v_b, slopes_b, lvec, gn_b, lam_init, tq_diff)
            yp = _outproj(yp, [o_a, o_b], gate, gate_p, w_out, fg, 1, tt_p)
            gla_p.append(s_new)
            dk_p.append(k_hd.reshape(bp, tp, H_B, 2 * DK_B))
            dv_p.append(v_hd.reshape(bp, tp, H_B, DV_B))

            dtypes_s = [F32] * len(dtypes)
            qkv_a, glow, q_b, k_b, v_b, gate = _inproj(ys, shift_s, scale_s, g_l, w_cat, splits_s, dtypes_s, nb_s, ts)
            o_a, s_new = _gla_sample(qkv_a, glow, state_gla[e], wg, bg, gn_a, min(8, bs_))
            ck = cache_diff_k.reshape(-1, page * H_B, 2 * DK_B)
            cv = cache_diff_v.reshape(-1, page * H_B, DV_B)
            o_b = _diff_sample(q_b, k_b, v_b, ck, cv, page_table, e * n_pool, lvec, gn_b, lam_init)
            ys = _outproj(ys, [o_a, o_b], gate, gate_s, w_out, fg, nb_s, ts)
            gla_s.append(s_new)
            dk_s.append(k_b.reshape(bs_, ts, H_B, 2 * DK_B))
            dv_s.append(v_b.reshape(bs_, ts, H_B, DV_B))
        else:
            o = l // 2
            w_cat = w_in_o[o].astype(BF16)
            splits = [(0, W_C, 1.0, 0), (W_C, W_C, 1.0, H_C), (2 * W_C, W_C, 1.0, H_C), (3 * W_C, W_C, 1.0, 0)]
            splits_s = [sp[:3] + (0,) for sp in splits]
            dtypes = [F32, F32, F32, F32]
            w_out = w_out_o[o].astype(BF16)

            q, k, v, gate, k_hd, v_hd = _inproj(yp, shift_p, scale_p, g_l, w_cat, splits, dtypes, 1, tt_p)
            att = _moba_prompt(q, k, v, slopes_c)
            yp = _outproj(yp, [att], gate, gate_p, w_out, fg, 1, tt_p)
            mk_p.append(k_hd.reshape(bp, tp, H_C, DH_C))
            mv_p.append(v_hd.reshape(bp, tp, H_C, DH_C))

            q, k, v, gate = _inproj(ys, shift_s, scale_s, g_l, w_cat, splits_s, dtypes, nb_s, ts)
            ck = cache_moba_k.reshape(-1, page * H_C, DH_C)
            cv = cache_moba_v.reshape(-1, page * H_C, DH_C)
            att = _moba_sample(q, k, v, ck, cv, page_table, o * n_pool)
            ys = _outproj(ys, [att], gate, gate_s, w_out, fg, nb_s, ts)
            mk_s.append(k.reshape(bs_, ts, H_C, DH_C))
            mv_s.append(v.reshape(bs_, ts, H_C, DH_C))
    return (yp, ys, jnp.stack(gla_p), jnp.stack(gla_s), jnp.stack(dk_p), jnp.stack(dv_p), jnp.stack(dk_s),
            jnp.stack(dv_s), jnp.stack(mk_p), jnp.stack(mv_p), jnp.stack(mk_s), jnp.stack(mv_s))
```

```python
import functools
import math

import jax
import jax.numpy as jnp
import numpy as np
from jax import lax
from jax.experimental import pallas as pl
from jax.experimental.pallas import tpu as pltpu

F32 = jnp.float32
BF16 = jnp.bfloat16

D_MODEL = 1024
H_A, DK_A, DV_A = 4, 64, 128
GLA_LOWRANK = 16
GLA_TAU = 16.0
GLA_CHUNK = 64
GLA_ATT_ROWS = 256
H_B, DK_B, DV_B = 4, 64, 128
H_C, DH_C = 8, 128
MOBA_BLOCK = 256
MOBA_TOPK = 3
W_A = H_A * DV_A
W_B = H_B * DV_B
W_C = H_C * DH_C
EPS = 1e-6
NEG = -1e30
LANE = 128
VT_ROWS = LANE + 16
KV_GROUP_KEYS = 512
FILL_TILE = 256
MOBA_Q_TILE = 512
VMEM_LIMIT = 56 * 1024 * 1024

HIGHEST = lax.Precision.HIGHEST


def _dot(a, b):
    return jnp.dot(a, b, preferred_element_type=F32)


def _dot_nt(a, b, precision=None):
    return lax.dot_general(a, b, (((1,), (1,)), ((), ())), preferred_element_type=F32, precision=precision)


def _dot_tn(a, b):
    return lax.dot_general(a, b, (((0,), (0,)), ((), ())), preferred_element_type=F32)


def _split_bf16(x, parts):
    out = []
    for _ in range(parts):
        p = x.astype(BF16)
        out.append(p)
        x = x - p.astype(F32)
    return out


def _dot_nt_split(a, b):
    ah, al = _split_bf16(a, 2)
    bh, bl = _split_bf16(b, 2)
    return _dot_nt(jnp.concatenate([ah, ah, al], axis=1), jnp.concatenate([bh, bl, bh], axis=1))


def _rms(x, g):
    return x * lax.rsqrt(jnp.mean(x * x, axis=-1, keepdims=True) + EPS) * g


def _silu(x):
    return x * jax.nn.sigmoid(x)


def _params(*sem):
    return pltpu.CompilerParams(dimension_semantics=sem, vmem_limit_bytes=VMEM_LIMIT)


def _ada_kernel(c_ref, w_ref, b_ref, o_ref):
    a = _silu(c_ref[...]).astype(BF16)
    o_ref[0] = _dot(a, w_ref[0].astype(BF16)) + b_ref[0]


def _ada(c_all, w_ada, b_ada):
    depth = w_ada.shape[0]
    mp = c_all.shape[0]
    tn = 1024
    return pl.pallas_call(
        _ada_kernel,
        out_shape=jax.ShapeDtypeStruct((depth, mp, 3 * D_MODEL), F32),
        grid=(depth, 3 * D_MODEL // tn),
        in_specs=[
            pl.BlockSpec((mp, D_MODEL), lambda l, j: (0, 0)),
            pl.BlockSpec((1, D_MODEL, tn), lambda l, j: (l, 0, j)),
            pl.BlockSpec((1, 1, tn), lambda l, j: (l, 0, j)),
        ],
        out_specs=pl.BlockSpec((1, mp, tn), lambda l, j: (l, 0, j)),
        compiler_params=_params("arbitrary", "arbitrary"),
        name="ada_mod",
    )(c_all, w_ada, b_ada.reshape(depth, 1, 3 * D_MODEL))


def _inproj_kernel(x_ref, shift_ref, scale_ref, g_ref, w_ref, *out_refs, splits):
    nb, tt, d = x_ref.shape
    x = x_ref[...]
    h = _rms(x, g_ref[...]) * (1.0 + scale_ref[...]) + shift_ref[...]
    h2 = h.reshape(nb * tt, d).astype(BF16)
    head_refs = list(out_refs[len(splits):])
    for (start, width, mult, n_heads), o_ref in zip(splits, out_refs):
        z = _dot(h2, w_ref[:, start:start + width])
        if mult != 1.0:
            z = z * mult
        o_ref[...] = z.reshape(nb, tt, width).astype(o_ref.dtype)
        if n_heads:
            hr = head_refs.pop(0)
            for hd in range(n_heads):
                hr.at[0][pl.ds(hd, tt, stride=n_heads), :] = z[:, hd * LANE:(hd + 1) * LANE]


def _inproj(x, shift, scale, g, w, splits, dtypes, nb, tt):
    nseq, t, d = x.shape
    n = w.shape[1]
    grid = (nseq // nb, t // tt)
    xmap = lambda i, j: (i, j, 0)
    out_shape = [jax.ShapeDtypeStruct((nseq, t, wd), dt) for (_, wd, _, _), dt in zip(splits, dtypes)]
    out_specs = [pl.BlockSpec((nb, tt, wd), xmap) for (_, wd, _, _) in splits]
    for (_, wd, _, nh) in splits:
        if nh:
            assert nb == 1 and wd == nh * LANE
            out_shape.append(jax.ShapeDtypeStruct((nseq, t * nh, LANE), F32))
            out_specs.append(pl.BlockSpec((1, tt * nh, LANE), xmap))
    return pl.pallas_call(
        functools.partial(_inproj_kernel, splits=tuple(splits)),
        out_shape=out_shape,
        grid=grid,
        in_specs=[
            pl.BlockSpec((nb, tt, d), xmap),
            pl.BlockSpec((nb, 1, d), lambda i, j: (i, 0, 0)),
            pl.BlockSpec((nb, 1, d), lambda i, j: (i, 0, 0)),
            pl.BlockSpec((1, 1, d), lambda i, j: (0, 0, 0)),
            pl.BlockSpec((d, n), lambda i, j: (0, 0)),
        ],
        out_specs=out_specs,
        compiler_params=_params("arbitrary", "arbitrary"),
        name="norm_inproj",
    )(x, shift, scale, g, w)


def _outproj_kernel(*refs, n_parts, final):
    x_ref = refs[0]
    o_refs = refs[1:1 + n_parts]
    gate_ref, ag_ref, w_ref = refs[1 + n_parts:4 + n_parts]
    rest = refs[4 + n_parts:]
    fg_ref = rest[0] if final else None
    y_ref = rest[-1]
    nb, tt, d = x_ref.shape
    rows = nb * tt
    sg = _silu(gate_ref[...].reshape(rows, -1))
    acc = jnp.zeros((rows, d), F32)
    col = 0
    for o_ref in o_refs:
        wd = o_ref.shape[-1]
        u = (o_ref[...].reshape(rows, wd) * sg[:, col:col + wd]).astype(BF16)
        acc = acc + _dot(u, w_ref[col:col + wd, :])
        col += wd
    y = x_ref[...] + ag_ref[...] * acc.reshape(nb, tt, d)
    if final:
        y = _rms(y, fg_ref[...])
    y_ref[...] = y


def _outproj(x, o_parts, gate, ada_gate, w, final_g, nb, tt):
    nseq, t, d = x.shape
    grid = (nseq // nb, t // tt)
    xmap = lambda i, j: (i, j, 0)
    final = final_g is not None
    in_specs = [pl.BlockSpec((nb, tt, d), xmap)]
    in_specs += [pl.BlockSpec((nb, tt, o.shape[-1]), xmap) for o in o_parts]
    in_specs += [
        pl.BlockSpec((nb, tt, gate.shape[-1]), xmap),
        pl.BlockSpec((nb, 1, d), lambda i, j: (i, 0, 0)),
        pl.BlockSpec(w.shape, lambda i, j: (0, 0)),
    ]
    args = [x, *o_parts, gate, ada_gate, w]
    if final:
        in_specs.append(pl.BlockSpec((1, 1, d), lambda i, j: (0, 0, 0)))
        args.append(final_g)
    return pl.pallas_call(
        functools.partial(_outproj_kernel, n_parts=len(o_parts), final=final),
        out_shape=jax.ShapeDtypeStruct((nseq, t, d), F32),
        grid=grid,
        in_specs=in_specs,
        out_specs=pl.BlockSpec((nb, tt, d), xmap),
        compiler_params=_params("arbitrary", "arbitrary"),
        name="gate_outproj",
    )(*args)


def _gla_rows(q, k, v, glow, wg, bg, gnorm, states, seq_len):
    r = q.shape[0]
    ns = r // seq_len
    la = jax.nn.log_sigmoid(_dot(glow.astype(BF16), wg) + bg) / GLA_TAU
    ri = lax.broadcasted_iota(jnp.int32, (r, r), 0)
    ci = lax.broadcasted_iota(jnp.int32, (r, r), 1)
    same = (ri // seq_len) == (ci // seq_len)
    causal = same & (ci <= ri)
    cum = jnp.dot(causal.astype(F32), la, preferred_element_type=F32, precision=HIGHEST)
    tot = jnp.dot(same.astype(F32), la, preferred_element_type=F32, precision=HIGHEST)
    q_all = q * (DK_A ** -0.5) * jnp.exp(cum)
    k_all = k * jnp.exp(-cum)
    ke_all = k * jnp.exp(tot - cum)
    dec = jnp.exp(tot)
    er = lax.broadcasted_iota(jnp.int32, (DK_A, DK_A), 0)
    ec = lax.broadcasted_iota(jnp.int32, (DK_A, DK_A), 1)
    eye = (er == ec).astype(F32)
    outs = []
    new_states = [[None] * H_A for _ in range(ns)]
    for h in range(H_A):
        ks = slice(h * DK_A, (h + 1) * DK_A)
        vs = slice(h * DV_A, (h + 1) * DV_A)
        q_in = q_all[:, ks]
        k_end = ke_all[:, ks]
        v_h = v[:, vs]
        att = jnp.where(causal, _dot_nt(q_in.astype(BF16), k_all[:, ks].astype(BF16)), 0.0)
        o = _dot(att.astype(BF16), v_h.astype(BF16))
        o_rows = []
        for j in range(ns):
            rs = slice(j * seq_len, (j + 1) * seq_len)
            s_old = states[j][h]
            o_rows.append(o[rs] + _dot(q_in[rs].astype(BF16), s_old.astype(BF16)))
            d_col = jnp.sum(eye * dec[j * seq_len:j * seq_len + 1, ks], axis=1, keepdims=True)
            new_states[j][h] = s_old * d_col + _dot_tn(k_end[rs].astype(BF16), v_h[rs].astype(BF16))
        o = o_rows[0] if ns == 1 else jnp.concatenate(o_rows, axis=0)
        outs.append(_rms(o, gnorm))
    return outs, new_states


def _gla_prompt_kernel(qkv_ref, glow_ref, wg_ref, bg_ref, gn_ref, o_ref, s_ref, st_sc):
    t = pl.program_id(1)
    tt = qkv_ref.shape[1]
    ch = GLA_CHUNK
    nc = tt // ch
    nq = H_A * DK_A

    @pl.when(t == 0)
    def _():
        st_sc[...] = jnp.zeros_like(st_sc)

    q = qkv_ref[0, :, 0:nq]
    k = qkv_ref[0, :, nq:2 * nq]
    la = jax.nn.log_sigmoid(_dot(glow_ref[0].astype(BF16), wg_ref[...]) + bg_ref[...]) / GLA_TAU
    la_w = jnp.concatenate([la[c * ch:(c + 1) * ch] for c in range(nc)], axis=1)
    ri = lax.broadcasted_iota(jnp.int32, (ch, ch), 0)
    ci = lax.broadcasted_iota(jnp.int32, (ch, ch), 1)
    tril = (ci <= ri).astype(BF16)
    cum_w = _dot(jnp.concatenate([tril] * 3, axis=1), jnp.concatenate(_split_bf16(la_w, 3), axis=0))
    tot_w = jnp.broadcast_to(cum_w[ch - 1:ch], cum_w.shape)
    tall = lambda a: jnp.concatenate([a[:, c * nq:(c + 1) * nq] for c in range(nc)], axis=0)
    cum = tall(cum_w)
    tot = tall(tot_w)
    q_all = q * (DK_A ** -0.5) * jnp.exp(cum)
    k_all = k * jnp.exp(-cum)
    ke_all = k * jnp.exp(tot - cum)
    dec = jnp.exp(tot)
    sub = min(GLA_ATT_ROWS, tt)
    rr = lax.broadcasted_iota(jnp.int32, (sub, sub), 0)
    cc = lax.broadcasted_iota(jnp.int32, (sub, sub), 1)
    causal = (rr // ch == cc // ch) & (cc <= rr)
    eye = (ri[:DK_A, :DK_A] == ci[:DK_A, :DK_A]).astype(F32)
    gn = gn_ref[...]
    for h in range(H_A):
        ks = slice(h * DK_A, (h + 1) * DK_A)
        q_in = q_all[:, ks].astype(BF16)
        k_end = ke_all[:, ks].astype(BF16)
        v_h = qkv_ref[0, :, 2 * nq + h * DV_A:2 * nq + (h + 1) * DV_A].astype(BF16)
        k_in = k_all[:, ks].astype(BF16)
        o_parts = []
        for s0 in range(0, tt, sub):
            ss = slice(s0, s0 + sub)
            att = jnp.where(causal, _dot_nt(q_in[ss], k_in[ss]), 0.0)
            o_parts.append(_dot(att.astype(BF16), v_h[ss]))
        o = jnp.concatenate(o_parts, axis=0)
        s_h = st_sc[h]
        rows_out = []
        for c in range(nc):
            rs = slice(c * ch, (c + 1) * ch)
            rows_out.append(o[rs] + _dot(q_in[rs], s_h.astype(BF16)))
            d_col = jnp.sum(eye * dec[c * ch:c * ch + 1, ks], axis=1, keepdims=True)
            s_h = s_h * d_col + _dot_tn(k_end[rs], v_h[rs])
        st_sc[h] = s_h
        o_ref[0, :, h * DV_A:(h + 1) * DV_A] = _rms(jnp.concatenate(rows_out, axis=0), gn)

    @pl.when(t == pl.num_programs(1) - 1)
    def _():
        s_ref[0] = st_sc[...]


def _gla_prompt(qkv, glow, wg, bg, gn, tt):
    b, t, _ = qkv.shape
    assert t % tt == 0 and tt % GLA_CHUNK == 0
    return pl.pallas_call(
        _gla_prompt_kernel,
        out_shape=[jax.ShapeDtypeStruct((b, t, W_A), F32), jax.ShapeDtypeStruct((b, H_A, DK_A, DV_A), F32)],
        grid=(b, t // tt),
        in_specs=[
            pl.BlockSpec((1, tt, qkv.shape[-1]), lambda i, j: (i, j, 0)),
            pl.BlockSpec((1, tt, LANE), lambda i, j: (i, j, 0)),
            pl.BlockSpec(wg.shape, lambda i, j: (0, 0)),
            pl.BlockSpec(bg.shape, lambda i, j: (0, 0)),
            pl.BlockSpec(gn.shape, lambda i, j: (0, 0)),
        ],
        out_specs=[
            pl.BlockSpec((1, tt, W_A), lambda i, j: (i, j, 0)),
            pl.BlockSpec((1, H_A, DK_A, DV_A), lambda i, j: (i, 0, 0, 0)),
        ],
        scratch_shapes=[pltpu.VMEM((H_A, DK_A, DV_A), F32)],
        compiler_params=_params("arbitrary", "arbitrary"),
        name="gla_prompt",
    )(qkv, glow, wg, bg, gn)


def _gla_sample_kernel(qkv_ref, glow_ref, s0_ref, wg_ref, bg_ref, gn_ref, o_ref, s_ref):
    nb, ts, _ = qkv_ref.shape
    nq = H_A * DK_A
    r = nb * ts
    qkv = qkv_ref[...].reshape(r, qkv_ref.shape[-1])
    glow = glow_ref[...].reshape(r, LANE)
    states = [[s0_ref[j, h] for h in range(H_A)] for j in range(nb)]
    outs, new_states = _gla_rows(qkv[:, 0:nq], qkv[:, nq:2 * nq], qkv[:, 2 * nq:2 * nq + W_A], glow,
                                 wg_ref[...], bg_ref[...], gn_ref[...], states, ts)
    for h in range(H_A):
        o_ref[:, :, h * DV_A:(h + 1) * DV_A] = outs[h].reshape(nb, ts, DV_A)
        for j in range(nb):
            s_ref[j, h] = new_states[j][h]


def _gla_sample(qkv, glow, s0, wg, bg, gn, nb):
    b, ts, _ = qkv.shape
    assert b % nb == 0 and ts % 8 == 0 and ts <= GLA_CHUNK
    return pl.pallas_call(
        _gla_sample_kernel,
        out_shape=[jax.ShapeDtypeStruct((b, ts, W_A), F32), jax.ShapeDtypeStruct((b, H_A, DK_A, DV_A), F32)],
        grid=(b // nb,),
        in_specs=[
            pl.BlockSpec((nb, ts, qkv.shape[-1]), lambda i: (i, 0, 0)),
            pl.BlockSpec((nb, ts, LANE), lambda i: (i, 0, 0)),
            pl.BlockSpec((nb, H_A, DK_A, DV_A), lambda i: (i, 0, 0, 0)),
            pl.BlockSpec(wg.shape, lambda i: (0, 0)),
            pl.BlockSpec(bg.shape, lambda i: (0, 0)),
            pl.BlockSpec(gn.shape, lambda i: (0, 0)),
        ],
        out_specs=[
            pl.BlockSpec((nb, ts, W_A), lambda i: (i, 0, 0)),
            pl.BlockSpec((nb, H_A, DK_A, DV_A), lambda i: (i, 0, 0, 0)),
        ],
        compiler_params=_params("arbitrary"),
        name="gla_sample",
    )(qkv, glow, s0, wg, bg, gn)


def _attend_cols(kaug_sc, vt_sc, q_aug, i, tq, acc_sc):
    ncol = q_aug.shape[0]
    t = kaug_sc.shape[0]
    gk = min(KV_GROUP_KEYS, t)
    assert gk % tq == 0 and t % gk == 0
    kr = lax.broadcasted_iota(jnp.int32, (gk, ncol), 0)
    qc = lax.broadcasted_iota(jnp.int32, (gk, ncol), 1) % tq
    for v in range(t // gk):
        head = v * gk

        @pl.when((i * tq) // gk == v)
        def _(head=head):
            s_t = _dot_nt(kaug_sc[head:head + gk, :], q_aug)
            s_t = jnp.where(kr + head <= qc + i * tq, s_t, NEG)
            m = jnp.max(s_t, axis=0, keepdims=True)
            if head > 0:
                s_h = _dot_nt(kaug_sc[0:head, :], q_aug)
                m = jnp.maximum(m, jnp.max(s_h, axis=0, keepdims=True))
            acc = _dot(vt_sc[:, head:head + gk], jnp.exp(s_t - m).astype(BF16))
            if head > 0:
                acc = acc + _dot(vt_sc[:, 0:head], jnp.exp(s_h - m).astype(BF16))
            acc_sc[...] = acc


def _fill_kv(k_ref, v_ref, kaug_sc, vt_sc, tk, aug_fn, km_sc=None):
    nblk = kaug_sc.shape[0] // tk
    dv = v_ref.shape[-1]
    rowv = lax.broadcasted_iota(jnp.int32, (VT_ROWS - dv, tk), 0)
    ones_rows = jnp.where(rowv == 0, 1.0, 0.0).astype(BF16)
    for n in range(nblk):
        rows = slice(n * tk, (n + 1) * tk)
        kt = k_ref[0, rows, :]
        if km_sc is not None:
            km_sc[n:n + 1, :] = jnp.mean(kt, axis=0, keepdims=True)
        kaug_sc[rows, 0:LANE] = kt.astype(BF16)
        kaug_sc[rows, LANE:] = aug_fn(n).astype(BF16)
        vt_sc[0:dv, rows] = v_ref[0, rows, :].T.astype(BF16)
        vt_sc[dv:, rows] = ones_rows


def _lambda(l_ref, lam_init):
    lv = l_ref[...]
    s1 = jnp.sum(lv[0:1] * lv[1:2], axis=1, keepdims=True)
    s2 = jnp.sum(lv[2:3] * lv[3:4], axis=1, keepdims=True)
    return jnp.exp(s1) - jnp.exp(s2) + lam_init


def _diff_prompt_kernel(slope_ref, q_ref, k_ref, v_ref, l_ref, gn_ref, o_ref, kaug_sc, vt_sc, acc_sc, *, lam_init):
    h = pl.program_id(1)
    i = pl.program_id(2)
    tq = q_ref.shape[1]

    @pl.when(i == 0)
    def _():
        slope = slope_ref[h]
        ft = min(FILL_TILE, tq)
        c = lax.broadcasted_iota(jnp.int32, (ft, LANE), 0).astype(F32)
        col = lax.broadcasted_iota(jnp.int32, (ft, LANE), 1)

        def aug(n):
            return jnp.where(col == 0, slope * float(ft * n), jnp.where(col == 1, slope * c, 0.0))

        _fill_kv(k_ref, v_ref, kaug_sc, vt_sc, ft, aug)

    q = q_ref[0]
    lane = lax.broadcasted_iota(jnp.int32, q.shape, 1)
    zero = jnp.zeros_like(q)
    ones2 = jnp.where(lane < 2, 1.0, 0.0).astype(BF16)
    q_aug = jnp.concatenate([
        jnp.concatenate([jnp.where(lane < DK_B, q, zero), ones2], axis=1),
        jnp.concatenate([jnp.where(lane >= DK_B, q, zero), ones2], axis=1)], axis=0)
    _attend_cols(kaug_sc, vt_sc, q_aug, i, tq, acc_sc)
    acc = acc_sc[...]
    o_t = acc[0:DV_B] / acc[DV_B:DV_B + 1]
    lam = _lambda(l_ref, lam_init)
    o = (o_t[:, :tq] - lam * o_t[:, tq:]).T
    o_ref[0] = _rms(o, gn_ref[...]) * (1.0 - lam_init)


def _diff_prompt(q, k, v, slopes, lvec, gn, lam_init, tq):
    b, t, _ = q.shape
    assert t % tq == 0
    nblk = t // tq
    kv_spec = pl.BlockSpec((1, t, LANE), lambda bi, h, i, s: (bi, 0, h))
    return pl.pallas_call(
        functools.partial(_diff_prompt_kernel, lam_init=lam_init),
        out_shape=jax.ShapeDtypeStruct((b, t, W_B), F32),
        grid_spec=pltpu.PrefetchScalarGridSpec(
            num_scalar_prefetch=1,
            grid=(b, H_B, nblk),
            in_specs=[
                pl.BlockSpec((1, tq, LANE), lambda bi, h, i, s: (bi, i, h)),
                kv_spec,
                kv_spec,
                pl.BlockSpec(lvec.shape, lambda bi, h, i, s: (0, 0)),
                pl.BlockSpec(gn.shape, lambda bi, h, i, s: (0, 0)),
            ],
            out_specs=pl.BlockSpec((1, tq, LANE), lambda bi, h, i, s: (bi, i, h)),
            scratch_shapes=[
                pltpu.VMEM((t, 2 * LANE), BF16),
                pltpu.VMEM((VT_ROWS, t), BF16),
                pltpu.VMEM((VT_ROWS, 2 * tq), F32),
            ],
        ),
        compiler_params=_params("arbitrary", "arbitrary", "arbitrary"),
        name="diff_prompt",
    )(slopes, q, k, v, lvec, gn)


def _topk_mask(g, valid, n_axis):
    nb = g.shape[n_axis]
    g = jnp.where(valid, g, -jnp.inf)
    idx = lax.broadcasted_iota(jnp.int32, g.shape, n_axis)
    rank = jnp.zeros(g.shape, jnp.int32)
    for m in range(nb):
        gm = lax.slice_in_dim(g, m, m + 1, axis=n_axis)
        beats = (gm > g) | ((gm == g) & (m < idx))
        rank = rank + beats.astype(jnp.int32)
    return (rank < MOBA_TOPK) & valid


def _moba_prompt_kernel(slope_ref, q_ref, k_ref, v_ref, o_ref, km_sc, kaug_sc, vt_sc, acc_sc):
    h = pl.program_id(1)
    i = pl.program_id(2)
    tq = q_ref.shape[1]
    bs = MOBA_BLOCK
    nblk = km_sc.shape[0]

    @pl.when(i == 0)
    def _():
        slope = slope_ref[h]
        c = lax.broadcasted_iota(jnp.int32, (bs, LANE), 0).astype(F32)
        col = lax.broadcasted_iota(jnp.int32, (bs, LANE), 1)

        def aug(n):
            return jnp.where(col == n, 1.0, jnp.where(col == nblk, slope * float(bs * n),
                                                      jnp.where(col == nblk + 1, slope * c, 0.0)))

        _fill_kv(k_ref, v_ref, kaug_sc, vt_sc, bs, aug, km_sc)

    q = q_ref[0]
    g = _dot_nt_split(km_sc[...], q)
    blk = lax.broadcasted_iota(jnp.int32, g.shape, 0)
    own = i * (tq // bs) + lax.broadcasted_iota(jnp.int32, g.shape, 1) // bs
    sel = _topk_mask(g, blk < own, 0)
    selb = jnp.where(sel | (blk == own), 0.0, NEG)
    row = lax.broadcasted_iota(jnp.int32, (LANE, tq), 0)
    coef = jnp.concatenate([selb, jnp.zeros((LANE - nblk, tq), F32)], axis=0)
    coef = jnp.where((row == nblk) | (row == nblk + 1), 1.0, coef)
    q_aug = jnp.concatenate([q * (DH_C ** -0.5), coef.T], axis=1).astype(BF16)
    _attend_cols(kaug_sc, vt_sc, q_aug, i, tq, acc_sc)
    acc = acc_sc[...]
    o_ref[0] = (acc[0:DH_C] / acc[DH_C:DH_C + 1]).T


def _moba_prompt(q, k, v, slopes):
    b, t, _ = q.shape
    bs = MOBA_BLOCK
    tq = min(MOBA_Q_TILE, t)
    assert t % tq == 0 and tq % bs == 0
    nblk = t // bs
    assert nblk + 2 <= LANE
    kv_spec = pl.BlockSpec((1, t, LANE), lambda bi, h, i, s: (bi, 0, h))
    q_spec = pl.BlockSpec((1, tq, LANE), lambda bi, h, i, s: (bi, i, h))
    return pl.pallas_call(
        _moba_prompt_kernel,
        out_shape=jax.ShapeDtypeStruct((b, t, W_C), F32),
        grid_spec=pltpu.PrefetchScalarGridSpec(
            num_scalar_prefetch=1,
            grid=(b, H_C, t // tq),
            in_specs=[q_spec, kv_spec, kv_spec],
            out_specs=q_spec,
            scratch_shapes=[
                pltpu.VMEM((nblk, DH_C), F32),
                pltpu.VMEM((t, 2 * LANE), BF16),
                pltpu.VMEM((VT_ROWS, t), BF16),
                pltpu.VMEM((VT_ROWS, tq), F32),
            ],
        ),
        compiler_params=_params("arbitrary", "arbitrary", "arbitrary"),
        name="moba_prompt",
    )(slopes, q, k, v)


def _score_cols_mask(n_col, n_heads, n_maps, dk, ts):
    c = np.arange(n_col)[:, None]
    f = np.arange(n_heads * n_maps * dk)[None, :]
    return (((c % LANE) // ts == f // (n_maps * dk)) & (c // LANE == (f % (n_maps * dk)) // dk)).astype(np.float32)


def _score_bias(n_col, n_heads, ts, past_len, lpad):
    j = np.arange(lpad)[:, None]
    c = np.arange(n_col)[None, :]
    hh = (c % LANE) // ts
    rel = past_len + c % ts - j
    slope = np.array([2.0 ** (-8.0 * (h + 1) / n_heads) for h in range(LANE // ts + 1)], np.float32)[hh]
    bias = np.where(hh < n_heads, -slope * rel.astype(np.float32), np.float32(0.0))
    return np.where((rel >= 0) & (j < past_len + ts), bias, np.float32(NEG)).astype(np.float32)


def _head_pair(p_ref, j, n_heads, page):
    rows = p_ref.at[0]
    return jnp.concatenate([rows[pl.ds(2 * j, page, stride=n_heads), :],
                            rows[pl.ds(2 * j + 1, page, stride=n_heads), :]], axis=1).astype(BF16)


def _page_scores(q, wmask_ref, kp_refs, kn_ref, s_sc, past_len, n_heads):
    ts, width = q.shape
    ncol = s_sc.shape[1]
    page = kp_refs[0].shape[1] // n_heads
    wt = (jnp.broadcast_to(q[None], (ncol // ts, ts, width)).reshape(ncol, width) * wmask_ref[...]).astype(BF16)
    for g, kp_ref in enumerate(kp_refs):
        acc = None
        for j in range(n_heads // 2):
            part = _dot_nt(_head_pair(kp_ref, j, n_heads, page), wt[:, 2 * j * LANE:(2 * j + 2) * LANE])
            acc = part if acc is None else acc + part
        s_sc[g * page:(g + 1) * page, :] = acc
    s_sc[past_len:past_len + ts, :] = _dot_nt(kn_ref[0].astype(BF16), wt)
    s_sc[past_len + ts:, :] = jnp.zeros((s_sc.shape[0] - past_len - ts, ncol), F32)


def _page_values(a_t, vp_refs, vn_ref, past_len, n_heads):
    page = vp_refs[0].shape[1] // n_heads
    ts, width = vn_ref.shape[1], vn_ref.shape[2]
    acc = [jnp.zeros((2 * ts, 2 * LANE), F32) for _ in range(n_heads // 2)]
    for g, vp_ref in enumerate(vp_refs):
        at = a_t[g * page:(g + 1) * page].T
        for j in range(n_heads // 2):
            acc[j] = acc[j] + _dot(at[2 * ts * j:2 * ts * (j + 1)].astype(BF16), _head_pair(vp_ref, j, n_heads, page))
    v_tail = jnp.concatenate([vn_ref[0], jnp.zeros((LANE - ts, width), F32)], axis=0)
    o_tail = _dot(a_t[past_len:].T.astype(BF16), v_tail.astype(BF16))
    outs = []
    for h in range(n_heads):
        j, r = divmod(h, 2)
        outs.append(acc[j][r * ts:(r + 1) * ts, r * LANE:(r + 1) * LANE]
                    + o_tail[h * ts:(h + 1) * ts, h * LANE:(h + 1) * LANE])
    return outs


def _softmax_rows(s):
    e = jnp.exp(s - jnp.max(s, axis=0, keepdims=True))
    return e * (1.0 / jnp.sum(e, axis=0, keepdims=True))


def _diff_sample_kernel(pt_ref, q_ref, kn_ref, vn_ref, wmask_ref, bias_ref, l_ref, gn_ref, *rest,
                        n_pages, lam_init, past_len):
    kp_refs, vp_refs = rest[:n_pages], rest[n_pages:2 * n_pages]
    o_ref, s_sc = rest[2 * n_pages:]
    ts = q_ref.shape[1]
    _page_scores(q_ref[0], wmask_ref, kp_refs, kn_ref, s_sc, past_len, H_B)
    pr = _softmax_rows(s_sc[...] + bias_ref[...])
    lam = _lambda(l_ref, lam_init)
    a_t = pr[:, :LANE] - lam * pr[:, LANE:]
    gn = gn_ref[...]
    for h, o in enumerate(_page_values(a_t, vp_refs, vn_ref, past_len, H_B)):
        o_ref[0, :, h * DV_B:(h + 1) * DV_B] = _rms(o, gn) * (1.0 - lam_init)


def _moba_sample_kernel(pt_ref, q_ref, kn_ref, vn_ref, wmask_ref, bias_ref, *rest, n_pages, past_len):
    kp_refs, vp_refs = rest[:n_pages], rest[n_pages:2 * n_pages]
    o_ref, s_sc = rest[2 * n_pages:]
    ts = q_ref.shape[1]
    bs = MOBA_BLOCK
    nbp = past_len // bs
    _page_scores(q_ref[0] * (DH_C ** -0.5), wmask_ref, kp_refs, kn_ref, s_sc, past_len, H_C)
    s = s_sc[...]
    blocks = [s[n * bs:(n + 1) * bs] for n in range(nbp)]
    g = jnp.concatenate([jnp.sum(b, axis=0, keepdims=True) for b in blocks], axis=0)
    selb = jnp.where(_topk_mask(g, jnp.full(g.shape, True), 0), 0.0, NEG)
    bias = bias_ref[...]
    parts = [blocks[n] + bias[n * bs:(n + 1) * bs] + selb[n:n + 1] for n in range(nbp)]
    parts.append(s[past_len:] + bias[past_len:])
    pr = _softmax_rows(jnp.concatenate(parts, axis=0))
    for h, o in enumerate(_page_values(pr, vp_refs, vn_ref, past_len, H_C)):
        o_ref[0, :, h * DH_C:(h + 1) * DH_C] = o


def _paged_call(body, name, q, k_new, v_new, cache_k, cache_v, page_table, page_base, n_heads, n_maps, extra):
    b, ts, width = q.shape
    n_pages = page_table.shape[1]
    page = cache_k.shape[1] // n_heads
    vwidth = v_new.shape[2]
    past_len = n_pages * page
    assert n_heads * ts <= LANE and ts % 8 == 0 and n_heads % 2 == 0
    assert width == n_heads * LANE and vwidth == n_heads * LANE and cache_k.shape[2] == LANE
    ncol = n_maps * LANE
    lpad = past_len + LANE
    wmask = jnp.asarray(_score_cols_mask(ncol, n_heads, n_maps, width // (n_heads * n_maps), ts))
    bias = jnp.asarray(_score_bias(ncol, n_heads, ts, past_len, lpad))
    const = lambda a: pl.BlockSpec(a.shape, lambda bi, pt: (0,) * a.ndim)
    row = lambda w: pl.BlockSpec((1, ts, w), lambda bi, pt: (bi, 0, 0))
    page_spec = lambda g: pl.BlockSpec((1, page * n_heads, LANE),
                                       lambda bi, pt: (page_base + pt[bi * n_pages + g], 0, 0))
    in_specs = [row(width), row(width), row(vwidth), const(wmask), const(bias)] + [const(a) for a in extra]
    in_specs += [page_spec(g) for g in range(n_pages)] * 2
    return pl.pallas_call(
        functools.partial(body, n_pages=n_pages, past_len=past_len),
        out_shape=jax.ShapeDtypeStruct((b, ts, vwidth), F32),
        grid_spec=pltpu.PrefetchScalarGridSpec(
            num_scalar_prefetch=1,
            grid=(b,),
            in_specs=in_specs,
            out_specs=row(vwidth),
            scratch_shapes=[pltpu.VMEM((lpad, ncol), F32)],
        ),
        compiler_params=_params("arbitrary"),
        name=name,
    )(page_table.reshape(-1), q, k_new, v_new, wmask, bias, *extra, *([cache_k] * n_pages), *([cache_v] * n_pages))


def _diff_sample(q, k_new, v_new, cache_k, cache_v, page_table, page_base, lvec, gn, lam_init):
    body = functools.partial(_diff_sample_kernel, lam_init=lam_init)
    return _paged_call(body, "diff_sample", q, k_new, v_new, cache_k, cache_v, page_table, page_base, H_B, 2,
                       [lvec, gn])


def _moba_sample(q, k_new, v_new, cache_k, cache_v, page_table, page_base):
    past_len = page_table.shape[1] * cache_k.shape[1] // H_C
    assert past_len % MOBA_BLOCK == 0 and past_len // MOBA_BLOCK >= MOBA_TOPK and q.shape[1] <= MOBA_BLOCK
    return _paged_call(_moba_sample_kernel, "moba_sample", q, k_new, v_new, cache_k, cache_v, page_table, page_base,
                       H_C, 1, [])


def _alibi_slopes(n):
    return jnp.array([2.0 ** (-8.0 * (h + 1) / n) for h in range(n)], dtype=F32)


def kernel(x_prompt, x_sample, c_prompt, c_sample, state_gla, cache_diff_k, cache_diff_v, cache_moba_k, cache_moba_v,
           page_table, norm_g, w_ada, b_ada, w_in_e, w_gla_gate, b_gla_gate, g_gla_norm, lam_q1, lam_k1, lam_q2,
           lam_k2, g_diff_norm, w_out_e, w_in_o, w_out_o, final_g):
    bp, tp, d = x_prompt.shape
    bs_, ts, _ = x_sample.shape
    depth = norm_g.shape[0]
    n_pool, page = cache_diff_k.shape[1], cache_diff_k.shape[2]
    assert d == D_MODEL

    tt_p = min(512, tp)
    nb_s = min(64, bs_)
    tq_diff = min(512, tp)

    mrows = bp + bs_
    mpad = -(-mrows // 8) * 8
    c_all = jnp.concatenate([c_prompt, c_sample, jnp.zeros((mpad - mrows, d), F32)], axis=0)
    mod = _ada(c_all, w_ada, b_ada)

    def mod_parts(l, lo, hi):
        m = mod[l, lo:hi]
        return m[:, None, 0:d], m[:, None, d:2 * d], m[:, None, 2 * d:3 * d]

    slopes_b = _alibi_slopes(H_B)
    slopes_c = _alibi_slopes(H_C)
    nqa = H_A * DK_A
    yp, ys = x_prompt, x_sample
    gla_p, gla_s, dk_p, dv_p, dk_s, dv_s, mk_p, mv_p, mk_s, mv_s = ([] for _ in range(10))
    for l in range(depth):
        shift_p, scale_p, gate_p = mod_parts(l, 0, bp)
        shift_s, scale_s, gate_s = mod_parts(l, bp, bp + bs_)
        g_l = norm_g[l].reshape(1, 1, d)
        last = l == depth - 1
        fg = final_g.reshape(1, 1, d) if last else None
        if l % 2 == 0:
            e = l // 2
            w = w_in_e[e]
            c0 = 2 * nqa + W_A
            c1 = c0 + GLA_LOWRANK
            w_cat = jnp.concatenate([w[:, :c0], w[:, c0:c1], jnp.zeros((d, LANE - GLA_LOWRANK), F32), w[:, c1:]],
                                    axis=1).astype(BF16)
            o0 = c0 + LANE
            splits = [(0, c0, 1.0, 0), (c0, LANE, 1.0, 0), (o0, 2 * H_B * DK_B, DK_B ** -0.5, 0),
                      (o0 + 512, 512, 1.0, H_B), (o0 + 1024, 512, 1.0, H_B), (o0 + 1536, W_A + W_B, 1.0, 0)]
            splits_s = [sp[:3] + (0,) for sp in splits]
            dtypes = [F32, F32, BF16, F32, F32, F32]
            wg = jnp.concatenate([w_gla_gate[e], jnp.zeros((LANE - GLA_LOWRANK, nqa), F32)], axis=0).astype(BF16)
            bg = b_gla_gate[e].reshape(1, nqa)
            gn_a = g_gla_norm[e].reshape(1, DV_A)
            gn_b = g_diff_norm[e].reshape(1, DV_B)
            lvec = jnp.stack([lam_q1[e], lam_k1[e], lam_q2[e], lam_k2[e]])
            lam_init = 0.8 - 0.6 * math.exp(-0.3 * l)
            w_out = w_out_e[e].astype(BF16)

            qkv_a, glow, q_b, k_b, v_b, gate, k_hd, v_hd = _inproj(yp, shift_p, scale_p, g_l, w_cat, splits, dtypes,
                                                                   1, tt_p)
            o_a, s_new = _gla_prompt(qkv_a, glow, wg, bg, gn_a, tt_p)
            o_b = _diff_prompt(q_b, k_b, v_b, slopes_b, lvec, gn_b, lam_init, tq_diff)
            yp = _outproj(yp, [o_a, o_b], gate, gate_p, w_out, fg, 1, tt_p)
            gla_p.append(s_new)
            dk_p.append(k_hd.reshape(bp, tp, H_B, 2 * DK_B))
            dv_p.append(v_hd.reshape(bp, tp, H_B, DV_B))

            dtypes_s = [F32] * len(dtypes)
            qkv_a, glow, q_b, k_b, v_b, gate = _inproj(ys, shift_s, scale_s, g_l, w_cat, splits_s, dtypes_s, nb_s, ts)
            o_a, s_new = _gla_sample(qkv_a, glow, state_gla[e], wg, bg, gn_a, min(8, bs_))
            ck = cache_diff_k.reshape(-1, page * H_B, 2 * DK_B)
            cv = cache_diff_v.reshape(-1, page * H_B, DV_B)
            o_b = _diff_sample(q_b, k_b, v_b, ck, cv, page_table, e * n_pool, lvec, gn_b, lam_init)
            ys = _outproj(ys, [o_a, o_b], gate, gate_s, w_out, fg, nb_s, ts)
            gla_s.append(s_new)
            dk_s.append(k_b.reshape(bs_, ts, H_B, 2 * DK_B))
            dv_s.append(v_b.reshape(bs_, ts, H_B, DV_B))
        else:
            o = l // 2
            w_cat = w_in_o[o].astype(BF16)
            splits = [(0, W_C, 1.0, 0), (W_C, W_C, 1.0, H_C), (2 * W_C, W_C, 1.0, H_C), (3 * W_C, W_C, 1.0, 0)]
            splits_s = [sp[:3] + (0,) for sp in splits]
            dtypes = [F32, F32, F32, F32]
            w_out = w_out_o[o].astype(BF16)

            q, k, v, gate, k_hd, v_hd = _inproj(yp, shift_p, scale_p, g_l, w_cat, splits, dtypes, 1, tt_p)
            att = _moba_prompt(q, k, v, slopes_c)
            yp = _outproj(yp, [att], gate, gate_p, w_out, fg, 1, tt_p)
            mk_p.append(k_hd.reshape(bp, tp, H_C, DH_C))
            mv_p.append(v_hd.reshape(bp, tp, H_C, DH_C))

            q, k, v, gate = _inproj(ys, shift_s, scale_s, g_l, w_cat, splits_s, dtypes, nb_s, ts)
            ck = cache_moba_k.reshape(-1, page * H_C, DH_C)
            cv = cache_moba_v.reshape(-1, page * H_C, DH_C)
            att = _moba_sample(q, k, v, ck, cv, page_table, o * n_pool)
            ys = _outproj(ys, [att], gate, gate_s, w_out, fg, nb_s, ts)
            mk_s.append(k.reshape(bs_, ts, H_C, DH_C))
            mv_s.append(v.reshape(bs_, ts, H_C, DH_C))
    return (yp, ys, jnp.stack(gla_p), jnp.stack(gla_s), jnp.stack(dk_p), jnp.stack(dv_p), jnp.stack(dk_s),
            jnp.stack(dv_s), jnp.stack(mk_p), jnp.stack(mv_p), jnp.stack(mk_s), jnp.stack(mv_s))
```

```python
import functools
import math

import jax
import jax.numpy as jnp
import numpy as np
from jax import lax
from jax.experimental import pallas as pl
from jax.experimental.pallas import tpu as pltpu

F32 = jnp.float32
BF16 = jnp.bfloat16

D_MODEL = 1024
H_A, DK_A, DV_A = 4, 64, 128
GLA_LOWRANK = 16
GLA_TAU = 16.0
GLA_CHUNK = 64
GLA_ATT_ROWS = 256
H_B, DK_B, DV_B = 4, 64, 128
H_C, DH_C = 8, 128
MOBA_BLOCK = 256
MOBA_TOPK = 3
W_A = H_A * DV_A
W_B = H_B * DV_B
W_C = H_C * DH_C
EPS = 1e-6
NEG = -1e30
LANE = 128
VT_ROWS = LANE + 16
KV_GROUP_KEYS = 512
FILL_TILE = 256
MOBA_Q_TILE = 512
VMEM_LIMIT = 56 * 1024 * 1024

HIGHEST = lax.Precision.HIGHEST


def _bf16_pieces(x, n):
    out, r = [], float(x)
    for _ in range(n):
        p = float(np.asarray(r, np.float32).astype(jnp.bfloat16).astype(np.float32))
        out.append(p)
        r -= p
    return out


LOG2E = math.log2(math.e)
LOG2E_PIECES = _bf16_pieces(LOG2E, 3)


def _dot(a, b):
    return jnp.dot(a, b, preferred_element_type=F32)


def _dot_nt(a, b, precision=None):
    return lax.dot_general(a, b, (((1,), (1,)), ((), ())), preferred_element_type=F32, precision=precision)


def _dot_tn(a, b):
    return lax.dot_general(a, b, (((0,), (0,)), ((), ())), preferred_element_type=F32)


def _split_bf16(x, parts):
    out = []
    for _ in range(parts):
        p = x.astype(BF16)
        out.append(p)
        x = x - p.astype(F32)
    return out


def _dot_nt_split(a, b):
    ah, al = _split_bf16(a, 2)
    bh, bl = _split_bf16(b, 2)
    return _dot_nt(jnp.concatenate([ah, ah, al], axis=1), jnp.concatenate([bh, bl, bh], axis=1))


def _rms(x, g):
    return x * lax.rsqrt(jnp.mean(x * x, axis=-1, keepdims=True) + EPS) * g


def _silu(x):
    return x * jax.nn.sigmoid(x)


def _params(*sem):
    return pltpu.CompilerParams(dimension_semantics=sem, vmem_limit_bytes=VMEM_LIMIT)


def _ada_kernel(c_ref, w_ref, b_ref, o_ref):
    a = _silu(c_ref[...]).astype(BF16)
    o_ref[0] = _dot(a, w_ref[0].astype(BF16)) + b_ref[0]


def _ada(c_all, w_ada, b_ada):
    depth = w_ada.shape[0]
    mp = c_all.shape[0]
    tn = 1024
    return pl.pallas_call(
        _ada_kernel,
        out_shape=jax.ShapeDtypeStruct((depth, mp, 3 * D_MODEL), F32),
        grid=(depth, 3 * D_MODEL // tn),
        in_specs=[
            pl.BlockSpec((mp, D_MODEL), lambda l, j: (0, 0)),
            pl.BlockSpec((1, D_MODEL, tn), lambda l, j: (l, 0, j)),
            pl.BlockSpec((1, 1, tn), lambda l, j: (l, 0, j)),
        ],
        out_specs=pl.BlockSpec((1, mp, tn), lambda l, j: (l, 0, j)),
        compiler_params=_params("arbitrary", "arbitrary"),
        name="ada_mod",
    )(c_all, w_ada, b_ada.reshape(depth, 1, 3 * D_MODEL))


def _inproj_kernel(x_ref, shift_ref, scale_ref, g_ref, w_ref, *out_refs, splits):
    nb, tt, d = x_ref.shape
    x = x_ref[...]
    h = _rms(x, g_ref[...]) * (1.0 + scale_ref[...]) + shift_ref[...]
    h2 = h.reshape(nb * tt, d).astype(BF16)
    head_refs = list(out_refs[len(splits):])
    for (start, width, mult, n_heads), o_ref in zip(splits, out_refs):
        z = _dot(h2, w_ref[:, start:start + width])
        if mult != 1.0:
            z = z * mult
        o_ref[...] = z.reshape(nb, tt, width).astype(o_ref.dtype)
        if n_heads:
            hr = head_refs.pop(0)
            for hd in range(n_heads):
                hr.at[0][pl.ds(hd, tt, stride=n_heads), :] = z[:, hd * LANE:(hd + 1) * LANE]


def _inproj(x, shift, scale, g, w, splits, dtypes, nb, tt):
    nseq, t, d = x.shape
    n = w.shape[1]
    grid = (nseq // nb, t // tt)
    xmap = lambda i, j: (i, j, 0)
    out_shape = [jax.ShapeDtypeStruct((nseq, t, wd), dt) for (_, wd, _, _), dt in zip(splits, dtypes)]
    out_specs = [pl.BlockSpec((nb, tt, wd), xmap) for (_, wd, _, _) in splits]
    for (_, wd, _, nh) in splits:
        if nh:
            assert nb == 1 and wd == nh * LANE
            out_shape.append(jax.ShapeDtypeStruct((nseq, t * nh, LANE), F32))
            out_specs.append(pl.BlockSpec((1, tt * nh, LANE), xmap))
    return pl.pallas_call(
        functools.partial(_inproj_kernel, splits=tuple(splits)),
        out_shape=out_shape,
        grid=grid,
        in_specs=[
            pl.BlockSpec((nb, tt, d), xmap),
            pl.BlockSpec((nb, 1, d), lambda i, j: (i, 0, 0)),
            pl.BlockSpec((nb, 1, d), lambda i, j: (i, 0, 0)),
            pl.BlockSpec((1, 1, d), lambda i, j: (0, 0, 0)),
            pl.BlockSpec((d, n), lambda i, j: (0, 0)),
        ],
        out_specs=out_specs,
        compiler_params=_params("arbitrary", "arbitrary"),
        name="norm_inproj",
    )(x, shift, scale, g, w)


def _outproj_kernel(*refs, n_parts, final):
    x_ref = refs[0]
    o_refs = refs[1:1 + n_parts]
    gate_ref, ag_ref, w_ref = refs[1 + n_parts:4 + n_parts]
    rest = refs[4 + n_parts:]
    fg_ref = rest[0] if final else None
    y_ref = rest[-1]
    nb, tt, d = x_ref.shape
    rows = nb * tt
    sg = _silu(gate_ref[...].reshape(rows, -1))
    acc = jnp.zeros((rows, d), F32)
    col = 0
    for o_ref in o_refs:
        wd = o_ref.shape[-1]
        u = (o_ref[...].reshape(rows, wd) * sg[:, col:col + wd]).astype(BF16)
        acc = acc + _dot(u, w_ref[col:col + wd, :])
        col += wd
    y = x_ref[...] + ag_ref[...] * acc.reshape(nb, tt, d)
    if final:
        y = _rms(y, fg_ref[...])
    y_ref[...] = y


def _outproj(x, o_parts, gate, ada_gate, w, final_g, nb, tt):
    nseq, t, d = x.shape
    grid = (nseq // nb, t // tt)
    xmap = lambda i, j: (i, j, 0)
    final = final_g is not None
    in_specs = [pl.BlockSpec((nb, tt, d), xmap)]
    in_specs += [pl.BlockSpec((nb, tt, o.shape[-1]), xmap) for o in o_parts]
    in_specs += [
        pl.BlockSpec((nb, tt, gate.shape[-1]), xmap),
        pl.BlockSpec((nb, 1, d), lambda i, j: (i, 0, 0)),
        pl.BlockSpec(w.shape, lambda i, j: (0, 0)),
    ]
    args = [x, *o_parts, gate, ada_gate, w]
    if final:
        in_specs.append(pl.BlockSpec((1, 1, d), lambda i, j: (0, 0, 0)))
        args.append(final_g)
    return pl.pallas_call(
        functools.partial(_outproj_kernel, n_parts=len(o_parts), final=final),
        out_shape=jax.ShapeDtypeStruct((nseq, t, d), F32),
        grid=grid,
        in_specs=in_specs,
        out_specs=pl.BlockSpec((nb, tt, d), xmap),
        compiler_params=_params("arbitrary", "arbitrary"),
        name="gate_outproj",
    )(*args)


def _gla_rows(q, k, v, glow, wg, bg, gnorm, states, seq_len):
    r = q.shape[0]
    ns = r // seq_len
    la = jax.nn.log_sigmoid(_dot(glow.astype(BF16), wg) + bg) / GLA_TAU
    ri = lax.broadcasted_iota(jnp.int32, (r, r), 0)
    ci = lax.broadcasted_iota(jnp.int32, (r, r), 1)
    same = (ri // seq_len) == (ci // seq_len)
    causal = same & (ci <= ri)
    cum = jnp.dot(causal.astype(F32), la, preferred_element_type=F32, precision=HIGHEST)
    tot = jnp.dot(same.astype(F32), la, preferred_element_type=F32, precision=HIGHEST)
    q_all = q * (DK_A ** -0.5) * jnp.exp(cum)
    k_all = k * jnp.exp(-cum)
    ke_all = k * jnp.exp(tot - cum)
    dec = jnp.exp(tot)
    er = lax.broadcasted_iota(jnp.int32, (DK_A, DK_A), 0)
    ec = lax.broadcasted_iota(jnp.int32, (DK_A, DK_A), 1)
    eye = (er == ec).astype(F32)
    outs = []
    new_states = [[None] * H_A for _ in range(ns)]
    for h in range(H_A):
        ks = slice(h * DK_A, (h + 1) * DK_A)
        vs = slice(h * DV_A, (h + 1) * DV_A)
        q_in = q_all[:, ks]
        k_end = ke_all[:, ks]
        v_h = v[:, vs]
        att = jnp.where(causal, _dot_nt(q_in.astype(BF16), k_all[:, ks].astype(BF16)), 0.0)
        o = _dot(att.astype(BF16), v_h.astype(BF16))
        o_rows = []
        for j in range(ns):
            rs = slice(j * seq_len, (j + 1) * seq_len)
            s_old = states[j][h]
            o_rows.append(o[rs] + _dot(q_in[rs].astype(BF16), s_old.astype(BF16)))
            d_col = jnp.sum(eye * dec[j * seq_len:j * seq_len + 1, ks], axis=1, keepdims=True)
            new_states[j][h] = s_old * d_col + _dot_tn(k_end[rs].astype(BF16), v_h[rs].astype(BF16))
        o = o_rows[0] if ns == 1 else jnp.concatenate(o_rows, axis=0)
        outs.append(_rms(o, gnorm))
    return outs, new_states


def _gla_prompt_kernel(qkv_ref, glow_ref, wg_ref, bg_ref, gn_ref, o_ref, s_ref, st_sc):
    t = pl.program_id(1)
    tt = qkv_ref.shape[1]
    ch = GLA_CHUNK
    nc = tt // ch
    nq = H_A * DK_A

    @pl.when(t == 0)
    def _():
        st_sc[...] = jnp.zeros_like(st_sc)

    q = qkv_ref[0, :, 0:nq]
    k = qkv_ref[0, :, nq:2 * nq]
    la = jax.nn.log_sigmoid(_dot(glow_ref[0].astype(BF16), wg_ref[...]) + bg_ref[...]) / GLA_TAU
    la_w = jnp.concatenate([la[c * ch:(c + 1) * ch] for c in range(nc)], axis=1)
    ri = lax.broadcasted_iota(jnp.int32, (ch, ch), 0)
    ci = lax.broadcasted_iota(jnp.int32, (ch, ch), 1)
    tril = (ci <= ri).astype(BF16)
    cum_w = _dot(jnp.concatenate([tril] * 3, axis=1), jnp.concatenate(_split_bf16(la_w, 3), axis=0))
    tot_w = jnp.broadcast_to(cum_w[ch - 1:ch], cum_w.shape)
    tall = lambda a: jnp.concatenate([a[:, c * nq:(c + 1) * nq] for c in range(nc)], axis=0)
    cum = tall(cum_w)
    tot = tall(tot_w)
    q_all = q * (DK_A ** -0.5) * jnp.exp(cum)
    k_all = k * jnp.exp(-cum)
    ke_all = k * jnp.exp(tot - cum)
    dec = jnp.exp(tot)
    sub = min(GLA_ATT_ROWS, tt)
    rr = lax.broadcasted_iota(jnp.int32, (sub, sub), 0)
    cc = lax.broadcasted_iota(jnp.int32, (sub, sub), 1)
    causal = (rr // ch == cc // ch) & (cc <= rr)
    eye = (ri[:DK_A, :DK_A] == ci[:DK_A, :DK_A]).astype(F32)
    gn = gn_ref[...]
    for h in range(H_A):
        ks = slice(h * DK_A, (h + 1) * DK_A)
        q_in = q_all[:, ks].astype(BF16)
        k_end = ke_all[:, ks].astype(BF16)
        v_h = qkv_ref[0, :, 2 * nq + h * DV_A:2 * nq + (h + 1) * DV_A].astype(BF16)
        k_in = k_all[:, ks].astype(BF16)
        o_parts = []
        for s0 in range(0, tt, sub):
            ss = slice(s0, s0 + sub)
            att = jnp.where(causal, _dot_nt(q_in[ss], k_in[ss]), 0.0)
            o_parts.append(_dot(att.astype(BF16), v_h[ss]))
        o = jnp.concatenate(o_parts, axis=0)
        s_h = st_sc[h]
        rows_out = []
        for c in range(nc):
            rs = slice(c * ch, (c + 1) * ch)
            rows_out.append(o[rs] + _dot(q_in[rs], s_h.astype(BF16)))
            d_col = jnp.sum(eye * dec[c * ch:c * ch + 1, ks], axis=1, keepdims=True)
            s_h = s_h * d_col + _dot_tn(k_end[rs], v_h[rs])
        st_sc[h] = s_h
        o_ref[0, :, h * DV_A:(h + 1) * DV_A] = _rms(jnp.concatenate(rows_out, axis=0), gn)

    @pl.when(t == pl.num_programs(1) - 1)
    def _():
        s_ref[0] = st_sc[...]


def _gla_prompt(qkv, glow, wg, bg, gn, tt):
    b, t, _ = qkv.shape
    assert t % tt == 0 and tt % GLA_CHUNK == 0
    return pl.pallas_call(
        _gla_prompt_kernel,
        out_shape=[jax.ShapeDtypeStruct((b, t, W_A), F32), jax.ShapeDtypeStruct((b, H_A, DK_A, DV_A), F32)],
        grid=(b, t // tt),
        in_specs=[
            pl.BlockSpec((1, tt, qkv.shape[-1]), lambda i, j: (i, j, 0)),
            pl.BlockSpec((1, tt, LANE), lambda i, j: (i, j, 0)),
            pl.BlockSpec(wg.shape, lambda i, j: (0, 0)),
            pl.BlockSpec(bg.shape, lambda i, j: (0, 0)),
            pl.BlockSpec(gn.shape, lambda i, j: (0, 0)),
        ],
        out_specs=[
            pl.BlockSpec((1, tt, W_A), lambda i, j: (i, j, 0)),
            pl.BlockSpec((1, H_A, DK_A, DV_A), lambda i, j: (i, 0, 0, 0)),
        ],
        scratch_shapes=[pltpu.VMEM((H_A, DK_A, DV_A), F32)],
        compiler_params=_params("arbitrary", "arbitrary"),
        name="gla_prompt",
    )(qkv, glow, wg, bg, gn)


def _gla_sample_kernel(qkv_ref, glow_ref, s0_ref, wg_ref, bg_ref, gn_ref, o_ref, s_ref):
    nb, ts, _ = qkv_ref.shape
    nq = H_A * DK_A
    r = nb * ts
    qkv = qkv_ref[...].reshape(r, qkv_ref.shape[-1])
    glow = glow_ref[...].reshape(r, LANE)
    states = [[s0_ref[j, h] for h in range(H_A)] for j in range(nb)]
    outs, new_states = _gla_rows(qkv[:, 0:nq], qkv[:, nq:2 * nq], qkv[:, 2 * nq:2 * nq + W_A], glow,
                                 wg_ref[...], bg_ref[...], gn_ref[...], states, ts)
    for h in range(H_A):
        o_ref[:, :, h * DV_A:(h + 1) * DV_A] = outs[h].reshape(nb, ts, DV_A)
        for j in range(nb):
            s_ref[j, h] = new_states[j][h]


def _gla_sample(qkv, glow, s0, wg, bg, gn, nb):
    b, ts, _ = qkv.shape
    assert b % nb == 0 and ts % 8 == 0 and ts <= GLA_CHUNK
    return pl.pallas_call(
        _gla_sample_kernel,
        out_shape=[jax.ShapeDtypeStruct((b, ts, W_A), F32), jax.ShapeDtypeStruct((b, H_A, DK_A, DV_A), F32)],
        grid=(b // nb,),
        in_specs=[
            pl.BlockSpec((nb, ts, qkv.shape[-1]), lambda i: (i, 0, 0)),
            pl.BlockSpec((nb, ts, LANE), lambda i: (i, 0, 0)),
            pl.BlockSpec((nb, H_A, DK_A, DV_A), lambda i: (i, 0, 0, 0)),
            pl.BlockSpec(wg.shape, lambda i: (0, 0)),
            pl.BlockSpec(bg.shape, lambda i: (0, 0)),
            pl.BlockSpec(gn.shape, lambda i: (0, 0)),
        ],
        out_specs=[
            pl.BlockSpec((nb, ts, W_A), lambda i: (i, 0, 0)),
            pl.BlockSpec((nb, H_A, DK_A, DV_A), lambda i: (i, 0, 0, 0)),
        ],
        compiler_params=_params("arbitrary"),
        name="gla_sample",
    )(qkv, glow, s0, wg, bg, gn)


def _attend_cols(kaug_sc, vt_sc, q_aug, i, tq, acc_sc):
    ncol = q_aug.shape[0]
    t = kaug_sc.shape[0]
    gk = min(KV_GROUP_KEYS, t)
    assert gk % tq == 0 and t % gk == 0
    kr = lax.broadcasted_iota(jnp.int32, (gk, ncol), 0)
    qc = lax.broadcasted_iota(jnp.int32, (gk, ncol), 1) % tq
    for v in range(t // gk):
        head = v * gk

        @pl.when((i * tq) // gk == v)
        def _(head=head):
            s_t = _dot_nt(kaug_sc[head:head + gk, :], q_aug)
            s_t = jnp.where(kr + head <= qc + i * tq, s_t, NEG)
            m = jnp.max(s_t, axis=0, keepdims=True)
            if head > 0:
                s_h = _dot_nt(kaug_sc[0:head, :], q_aug)
                m = jnp.maximum(m, jnp.max(s_h, axis=0, keepdims=True))
            acc = _dot(vt_sc[:, head:head + gk], jnp.exp2(s_t - m).astype(BF16))
            if head > 0:
                acc = acc + _dot(vt_sc[:, 0:head], jnp.exp2(s_h - m).astype(BF16))
            acc_sc[...] = acc


def _log2e_coef(idx, base):
    r = idx - base
    out = jnp.zeros(idx.shape, F32)
    for n, piece in enumerate(LOG2E_PIECES):
        out = jnp.where((r == n) | (r == n + 3), piece, out)
    return out


def _fill_kv(k_ref, v_ref, kaug_sc, vt_sc, tk, aug_fn, km_sc=None):
    nblk = kaug_sc.shape[0] // tk
    dv = v_ref.shape[-1]
    rowv = lax.broadcasted_iota(jnp.int32, (VT_ROWS - dv, tk), 0)
    ones_rows = jnp.where(rowv == 0, 1.0, 0.0).astype(BF16)
    for n in range(nblk):
        rows = slice(n * tk, (n + 1) * tk)
        kt = k_ref[0, rows, :]
        if km_sc is not None:
            km_sc[n:n + 1, :] = jnp.mean(kt, axis=0, keepdims=True)
        kaug_sc[rows, 0:LANE] = kt.astype(BF16)
        kaug_sc[rows, LANE:] = aug_fn(n).astype(BF16)
        vt_sc[0:dv, rows] = v_ref[0, rows, :].T.astype(BF16)
        vt_sc[dv:, rows] = ones_rows


def _lambda(l_ref, lam_init):
    lv = l_ref[...]
    s1 = jnp.sum(lv[0:1] * lv[1:2], axis=1, keepdims=True)
    s2 = jnp.sum(lv[2:3] * lv[3:4], axis=1, keepdims=True)
    return jnp.exp(s1) - jnp.exp(s2) + lam_init


def _diff_prompt_kernel(slope_ref, q_ref, k_ref, v_ref, l_ref, gn_ref, o_ref, kaug_sc, vt_sc, acc_sc, *, lam_init):
    h = pl.program_id(1)
    i = pl.program_id(2)
    tq = q_ref.shape[1]

    @pl.when(i == 0)
    def _():
        slope = slope_ref[h]
        ft = min(FILL_TILE, tq)
        c = lax.broadcasted_iota(jnp.int32, (ft, LANE), 0).astype(F32)
        col = lax.broadcasted_iota(jnp.int32, (ft, LANE), 1)

        def aug(n):
            return jnp.where(col < 3, slope * float(ft * n), jnp.where(col < 6, slope * c, 0.0))

        _fill_kv(k_ref, v_ref, kaug_sc, vt_sc, ft, aug)

    q = q_ref[0]
    lane = lax.broadcasted_iota(jnp.int32, q.shape, 1)
    zero = jnp.zeros_like(q)
    lcoef = _log2e_coef(lane, 0).astype(BF16)
    q_aug = jnp.concatenate([
        jnp.concatenate([jnp.where(lane < DK_B, q, zero), lcoef], axis=1),
        jnp.concatenate([jnp.where(lane >= DK_B, q, zero), lcoef], axis=1)], axis=0)
    _attend_cols(kaug_sc, vt_sc, q_aug, i, tq, acc_sc)
    acc = acc_sc[...]
    o_t = acc[0:DV_B] / acc[DV_B:DV_B + 1]
    lam = _lambda(l_ref, lam_init)
    o = (o_t[:, :tq] - lam * o_t[:, tq:]).T
    o_ref[0] = _rms(o, gn_ref[...]) * (1.0 - lam_init)


def _diff_prompt(q, k, v, slopes, lvec, gn, lam_init, tq):
    b, t, _ = q.shape
    assert t % tq == 0
    nblk = t // tq
    kv_spec = pl.BlockSpec((1, t, LANE), lambda bi, h, i, s: (bi, 0, h))
    return pl.pallas_call(
        functools.partial(_diff_prompt_kernel, lam_init=lam_init),
        out_shape=jax.ShapeDtypeStruct((b, t, W_B), F32),
        grid_spec=pltpu.PrefetchScalarGridSpec(
            num_scalar_prefetch=1,
            grid=(b, H_B, nblk),
            in_specs=[
                pl.BlockSpec((1, tq, LANE), lambda bi, h, i, s: (bi, i, h)),
                kv_spec,
                kv_spec,
                pl.BlockSpec(lvec.shape, lambda bi, h, i, s: (0, 0)),
                pl.BlockSpec(gn.shape, lambda bi, h, i, s: (0, 0)),
            ],
            out_specs=pl.BlockSpec((1, tq, LANE), lambda bi, h, i, s: (bi, i, h)),
            scratch_shapes=[
                pltpu.VMEM((t, 2 * LANE), BF16),
                pltpu.VMEM((VT_ROWS, t), BF16),
                pltpu.VMEM((VT_ROWS, 2 * tq), F32),
            ],
        ),
        compiler_params=_params("arbitrary", "arbitrary", "arbitrary"),
        name="diff_prompt",
    )(slopes, q, k, v, lvec, gn)


def _topk_mask(g, valid, n_axis):
    nb = g.shape[n_axis]
    g = jnp.where(valid, g, -jnp.inf)
    idx = lax.broadcasted_iota(jnp.int32, g.shape, n_axis)
    rank = jnp.zeros(g.shape, jnp.int32)
    for m in range(nb):
        gm = lax.slice_in_dim(g, m, m + 1, axis=n_axis)
        beats = (gm > g) | ((gm == g) & (m < idx))
        rank = rank + beats.astype(jnp.int32)
    return (rank < MOBA_TOPK) & valid


def _moba_prompt_kernel(slope_ref, q_ref, k_ref, v_ref, o_ref, km_sc, kaug_sc, vt_sc, acc_sc):
    h = pl.program_id(1)
    i = pl.program_id(2)
    tq = q_ref.shape[1]
    bs = MOBA_BLOCK
    nblk = km_sc.shape[0]

    @pl.when(i == 0)
    def _():
        slope = slope_ref[h]
        c = lax.broadcasted_iota(jnp.int32, (bs, LANE), 0).astype(F32)
        col = lax.broadcasted_iota(jnp.int32, (bs, LANE), 1)

        def aug(n):
            return jnp.where(col == n, 1.0, jnp.where(col < nblk, 0.0, jnp.where(
                col < nblk + 3, slope * float(bs * n), jnp.where(col < nblk + 6, slope * c, 0.0))))

        _fill_kv(k_ref, v_ref, kaug_sc, vt_sc, bs, aug, km_sc)

    q = q_ref[0]
    g = _dot_nt_split(km_sc[...], q)
    blk = lax.broadcasted_iota(jnp.int32, g.shape, 0)
    own = i * (tq // bs) + lax.broadcasted_iota(jnp.int32, g.shape, 1) // bs
    sel = _topk_mask(g, blk < own, 0)
    selb = jnp.where(sel | (blk == own), 0.0, NEG)
    row = lax.broadcasted_iota(jnp.int32, (LANE, tq), 0)
    coef = jnp.concatenate([selb, jnp.zeros((LANE - nblk, tq), F32)], axis=0)
    coef = coef + _log2e_coef(row, nblk)
    q_aug = jnp.concatenate([q * (DH_C ** -0.5 * LOG2E), coef.T], axis=1).astype(BF16)
    _attend_cols(kaug_sc, vt_sc, q_aug, i, tq, acc_sc)
    acc = acc_sc[...]
    o_ref[0] = (acc[0:DH_C] / acc[DH_C:DH_C + 1]).T


def _moba_prompt(q, k, v, slopes):
    b, t, _ = q.shape
    bs = MOBA_BLOCK
    tq = min(MOBA_Q_TILE, t)
    assert t % tq == 0 and tq % bs == 0
    nblk = t // bs
    assert nblk + 6 <= LANE
    kv_spec = pl.BlockSpec((1, t, LANE), lambda bi, h, i, s: (bi, 0, h))
    q_spec = pl.BlockSpec((1, tq, LANE), lambda bi, h, i, s: (bi, i, h))
    return pl.pallas_call(
        _moba_prompt_kernel,
        out_shape=jax.ShapeDtypeStruct((b, t, W_C), F32),
        grid_spec=pltpu.PrefetchScalarGridSpec(
            num_scalar_prefetch=1,
            grid=(b, H_C, t // tq),
            in_specs=[q_spec, kv_spec, kv_spec],
            out_specs=q_spec,
            scratch_shapes=[
                pltpu.VMEM((nblk, DH_C), F32),
                pltpu.VMEM((t, 2 * LANE), BF16),
                pltpu.VMEM((VT_ROWS, t), BF16),
                pltpu.VMEM((VT_ROWS, tq), F32),
            ],
        ),
        compiler_params=_params("arbitrary", "arbitrary", "arbitrary"),
        name="moba_prompt",
    )(slopes, q, k, v)


def _score_cols_mask(n_heads, n_maps, dk, ts):
    c = np.arange(LANE)[:, None]
    f = np.arange(n_heads * n_maps * dk)[None, :]
    ms = LANE // n_maps
    return (((c % ms) // ts == f // (n_maps * dk)) & (c // ms == (f % (n_maps * dk)) // dk)).astype(np.float32)


def _score_bias(n_heads, n_maps, ts, past_len, lpad):
    j = np.arange(lpad)[:, None]
    c = np.arange(LANE)[None, :]
    hh = (c % (LANE // n_maps)) // ts
    rel = past_len + c % ts - j
    slope = np.array([2.0 ** (-8.0 * (h + 1) / n_heads) for h in range(LANE // ts + 1)], np.float32)[hh]
    bias = np.where(hh < n_heads, -slope * rel.astype(np.float32), np.float32(0.0))
    return np.where((rel >= 0) & (j < past_len + ts), bias, np.float32(NEG)).astype(np.float32)


def _head_pair(p_ref, j, n_heads, page):
    rows = p_ref.at[0]
    return jnp.concatenate([rows[pl.ds(2 * j, page, stride=n_heads), :],
                            rows[pl.ds(2 * j + 1, page, stride=n_heads), :]], axis=1).astype(BF16)


def _page_scores(q, wmask_ref, kp_refs, kn_ref, s_sc, past_len, n_heads):
    ts, width = q.shape
    ncol = s_sc.shape[1]
    page = kp_refs[0].shape[1] // n_heads
    wt = (jnp.broadcast_to(q[None], (ncol // ts, ts, width)).reshape(ncol, width) * wmask_ref[...]).astype(BF16)
    for g, kp_ref in enumerate(kp_refs):
        acc = None
        for j in range(n_heads // 2):
            part = _dot_nt(_head_pair(kp_ref, j, n_heads, page), wt[:, 2 * j * LANE:(2 * j + 2) * LANE])
            acc = part if acc is None else acc + part
        s_sc[g * page:(g + 1) * page, :] = acc
    s_sc[past_len:past_len + ts, :] = _dot_nt(kn_ref[0].astype(BF16), wt)
    s_sc[past_len + ts:, :] = jnp.zeros((s_sc.shape[0] - past_len - ts, ncol), F32)


def _page_values(a_t, vp_refs, vn_ref, past_len, n_heads):
    page = vp_refs[0].shape[1] // n_heads
    ts, width = vn_ref.shape[1], vn_ref.shape[2]
    acc = [jnp.zeros((2 * ts, 2 * LANE), F32) for _ in range(n_heads // 2)]
    for g, vp_ref in enumerate(vp_refs):
        at = a_t[g * page:(g + 1) * page].T
        for j in range(n_heads // 2):
            acc[j] = acc[j] + _dot(at[2 * ts * j:2 * ts * (j + 1)].astype(BF16), _head_pair(vp_ref, j, n_heads, page))
    v_tail = jnp.concatenate([vn_ref[0], jnp.zeros((LANE - ts, width), F32)], axis=0)
    o_tail = _dot(a_t[past_len:].T.astype(BF16), v_tail.astype(BF16))
    outs = []
    for h in range(n_heads):
        j, r = divmod(h, 2)
        outs.append(acc[j][r * ts:(r + 1) * ts, r * LANE:(r + 1) * LANE]
                    + o_tail[h * ts:(h + 1) * ts, h * LANE:(h + 1) * LANE])
    return outs


def _softmax_rows(s):
    e = jnp.exp(s - jnp.max(s, axis=0, keepdims=True))
    return e * (1.0 / jnp.sum(e, axis=0, keepdims=True))


def _diff_sample_kernel(pt_ref, q_ref, kn_ref, vn_ref, wmask_ref, bias_ref, l_ref, gn_ref, *rest,
                        n_pages, lam_init, past_len):
    kp_refs, vp_refs = rest[:n_pages], rest[n_pages:2 * n_pages]
    o_ref, s_sc = rest[2 * n_pages:]
    ts = q_ref.shape[1]
    _page_scores(q_ref[0], wmask_ref, kp_refs, kn_ref, s_sc, past_len, H_B)
    pr = _softmax_rows(s_sc[...] + bias_ref[...])
    lam = _lambda(l_ref, lam_init)
    a_t = pr - lam * pltpu.roll(pr, LANE // 2, axis=1)
    gn = gn_ref[...]
    for h, o in enumerate(_page_values(a_t, vp_refs, vn_ref, past_len, H_B)):
        o_ref[0, :, h * DV_B:(h + 1) * DV_B] = _rms(o, gn) * (1.0 - lam_init)


def _moba_sample_kernel(pt_ref, q_ref, kn_ref, vn_ref, wmask_ref, bias_ref, *rest, n_pages, past_len):
    kp_refs, vp_refs = rest[:n_pages], rest[n_pages:2 * n_pages]
    o_ref, s_sc = rest[2 * n_pages:]
    ts = q_ref.shape[1]
    bs = MOBA_BLOCK
    nbp = past_len // bs
    _page_scores(q_ref[0] * (DH_C ** -0.5), wmask_ref, kp_refs, kn_ref, s_sc, past_len, H_C)
    s = s_sc[...]
    blocks = [s[n * bs:(n + 1) * bs] for n in range(nbp)]
    g = jnp.concatenate([jnp.sum(b, axis=0, keepdims=True) for b in blocks], axis=0)
    selb = jnp.where(_topk_mask(g, jnp.full(g.shape, True), 0), 0.0, NEG)
    bias = bias_ref[...]
    parts = [blocks[n] + bias[n * bs:(n + 1) * bs] + selb[n:n + 1] for n in range(nbp)]
    parts.append(s[past_len:] + bias[past_len:])
    pr = _softmax_rows(jnp.concatenate(parts, axis=0))
    for h, o in enumerate(_page_values(pr, vp_refs, vn_ref, past_len, H_C)):
        o_ref[0, :, h * DH_C:(h + 1) * DH_C] = o


def _paged_call(body, name, q, k_new, v_new, cache_k, cache_v, page_table, page_base, n_heads, n_maps, extra):
    b, ts, width = q.shape
    n_pages = page_table.shape[1]
    page = cache_k.shape[1] // n_heads
    vwidth = v_new.shape[2]
    past_len = n_pages * page
    assert n_maps * n_heads * ts <= LANE and ts % 8 == 0 and n_heads % 2 == 0
    assert width == n_heads * LANE and vwidth == n_heads * LANE and cache_k.shape[2] == LANE
    ncol = LANE
    lpad = past_len + LANE
    wmask = jnp.asarray(_score_cols_mask(n_heads, n_maps, width // (n_heads * n_maps), ts))
    bias = jnp.asarray(_score_bias(n_heads, n_maps, ts, past_len, lpad))
    const = lambda a: pl.BlockSpec(a.shape, lambda bi, pt: (0,) * a.ndim)
    row = lambda w: pl.BlockSpec((1, ts, w), lambda bi, pt: (bi, 0, 0))
    page_spec = lambda g: pl.BlockSpec((1, page * n_heads, LANE),
                                       lambda bi, pt: (page_base + pt[bi * n_pages + g], 0, 0))
    in_specs = [row(width), row(width), row(vwidth), const(wmask), const(bias)] + [const(a) for a in extra]
    in_specs += [page_spec(g) for g in range(n_pages)] * 2
    return pl.pallas_call(
        functools.partial(body, n_pages=n_pages, past_len=past_len),
        out_shape=jax.ShapeDtypeStruct((b, ts, vwidth), F32),
        grid_spec=pltpu.PrefetchScalarGridSpec(
            num_scalar_prefetch=1,
            grid=(b,),
            in_specs=in_specs,
            out_specs=row(vwidth),
            scratch_shapes=[pltpu.VMEM((lpad, ncol), F32)],
        ),
        compiler_params=_params("arbitrary"),
        name=name,
    )(page_table.reshape(-1), q, k_new, v_new, wmask, bias, *extra, *([cache_k] * n_pages), *([cache_v] * n_pages))


def _diff_sample(q, k_new, v_new, cache_k, cache_v, page_table, page_base, lvec, gn, lam_init):
    body = functools.partial(_diff_sample_kernel, lam_init=lam_init)
    return _paged_call(body, "diff_sample", q, k_new, v_new, cache_k, cache_v, page_table, page_base, H_B, 2,
                       [lvec, gn])


def _moba_sample(q, k_new, v_new, cache_k, cache_v, page_table, page_base):
    past_len = page_table.shape[1] * cache_k.shape[1] // H_C
    assert past_len % MOBA_BLOCK == 0 and past_len // MOBA_BLOCK >= MOBA_TOPK and q.shape[1] <= MOBA_BLOCK
    return _paged_call(_moba_sample_kernel, "moba_sample", q, k_new, v_new, cache_k, cache_v, page_table, page_base,
                       H_C, 1, [])


def _alibi_slopes(n):
    return jnp.array([2.0 ** (-8.0 * (h + 1) / n) for h in range(n)], dtype=F32)


def kernel(x_prompt, x_sample, c_prompt, c_sample, state_gla, cache_diff_k, cache_diff_v, cache_moba_k, cache_moba_v,
           page_table, norm_g, w_ada, b_ada, w_in_e, w_gla_gate, b_gla_gate, g_gla_norm, lam_q1, lam_k1, lam_q2,
           lam_k2, g_diff_norm, w_out_e, w_in_o, w_out_o, final_g):
    bp, tp, d = x_prompt.shape
    bs_, ts, _ = x_sample.shape
    depth = norm_g.shape[0]
    n_pool, page = cache_diff_k.shape[1], cache_diff_k.shape[2]
    assert d == D_MODEL

    tt_p = min(512, tp)
    nb_s = min(64, bs_)
    tq_diff = min(512, tp)

    mrows = bp + bs_
    mpad = -(-mrows // 8) * 8
    c_all = jnp.concatenate([c_prompt, c_sample, jnp.zeros((mpad - mrows, d), F32)], axis=0)
    mod = _ada(c_all, w_ada, b_ada)

    def mod_parts(l, lo, hi):
        m = mod[l, lo:hi]
        return m[:, None, 0:d], m[:, None, d:2 * d], m[:, None, 2 * d:3 * d]

    slopes_b = _alibi_slopes(H_B)
    slopes_c = _alibi_slopes(H_C)
    nqa = H_A * DK_A
    yp, ys = x_prompt, x_sample
    gla_p, gla_s, dk_p, dv_p, dk_s, dv_s, mk_p, mv_p, mk_s, mv_s = ([] for _ in range(10))
    for l in range(depth):
        shift_p, scale_p, gate_p = mod_parts(l, 0, bp)
        shift_s, scale_s, gate_s = mod_parts(l, bp, bp + bs_)
        g_l = norm_g[l].reshape(1, 1, d)
        last = l == depth - 1
        fg = final_g.reshape(1, 1, d) if last else None
        if l % 2 == 0:
            e = l // 2
            w = w_in_e[e]
            c0 = 2 * nqa + W_A
            c1 = c0 + GLA_LOWRANK
            w_cat = jnp.concatenate([w[:, :c0], w[:, c0:c1], jnp.zeros((d, LANE - GLA_LOWRANK), F32), w[:, c1:]],
                                    axis=1).astype(BF16)
            o0 = c0 + LANE
            qmul = DK_B ** -0.5
            splits_s = [(0, c0, 1.0, 0), (c0, LANE, 1.0, 0), (o0, 2 * H_B * DK_B, qmul, 0),
                        (o0 + 512, 512, 1.0, 0), (o0 + 1024, 512, 1.0, 0), (o0 + 1536, W_A + W_B, 1.0, 0)]
            splits = [sp[:3] + (nh,) for sp, nh in zip(splits_s, (0, 0, 0, H_B, H_B, 0))]
            splits[2] = splits[2][:2] + (qmul * LOG2E, 0)
            dtypes = [F32, F32, BF16, F32, F32, F32]
            wg = jnp.concatenate([w_gla_gate[e], jnp.zeros((LANE - GLA_LOWRANK, nqa), F32)], axis=0).astype(BF16)
            bg = b_gla_gate[e].reshape(1, nqa)
            gn_a = g_gla_norm[e].reshape(1, DV_A)
            gn_b = g_diff_norm[e].reshape(1, DV_B)
            lvec = jnp.stack([lam_q1[e], lam_k1[e], lam_q2[e], lam_k2[e]])
            lam_init = 0.8 - 0.6 * math.exp(-0.3 * l)
            w_out = w_out_e[e].astype(BF16)

            qkv_a, glow, q_b, k_b, v_b, gate, k_hd, v_hd = _inproj(yp, shift_p, scale_p, g_l, w_cat, splits, dtypes,
                                                                   1, tt_p)
            o_a, s_new = _gla_prompt(qkv_a, glow, wg, bg, gn_a, tt_p)
            o_b = _diff_prompt(q_b, k_b, v_b, slopes_b, lvec, gn_b, lam_init, tq_diff)
            yp = _outproj(yp, [o_a, o_b], gate, gate_p, w_out, fg, 1, tt_p)
            gla_p.append(s_new)
            dk_p.append(k_hd.reshape(bp, tp, H_B, 2 * DK_B))
            dv_p.append(v_hd.reshape(bp, tp, H_B, DV_B))

            dtypes_s = [F32] * len(dtypes)
            qkv_a, glow, q_b, k_b, v_b, gate = _inproj(ys, shift_s, scale_s, g_l, w_cat, splits_s, dtypes_s, nb_s, ts)
            o_a, s_new = _gla_sample(qkv_a, glow, state_gla[e], wg, bg, gn_a, min(8, bs_))
            ck = cache_diff_k.reshape(-1, page * H_B, 2 * DK_B)
            cv = cache_diff_v.reshape(-1, page * H_B, DV_B)
            o_b = _diff_sample(q_b, k_b, v_b, ck, cv, page_table, e * n_pool, lvec, gn_b, lam_init)
            ys = _outproj(ys, [o_a, o_b], gate, gate_s, w_out, fg, nb_s, ts)
            gla_s.append(s_new)
            dk_s.append(k_b.reshape(bs_, ts, H_B, 2 * DK_B))
            dv_s.append(v_b.reshape(bs_, ts, H_B, DV_B))
        else:
            o = l // 2
            w_cat = w_in_o[o].astype(BF16)
            splits = [(0, W_C, 1.0, 0), (W_C, W_C, 1.0, H_C), (2 * W_C, W_C, 1.0, H_C), (3 * W_C, W_C, 1.0, 0)]
            splits_s = [sp[:3] + (0,) for sp in splits]
            dtypes = [F32, F32, F32, F32]
            w_out = w_out_o[o].astype(BF16)

            q, k, v, gate, k_hd, v_hd = _inproj(yp, shift_p, scale_p, g_l, w_cat, splits, dtypes, 1, tt_p)
            att = _moba_prompt(q, k, v, slopes_c)
            yp = _outproj(yp, [att], gate, gate_p, w_out, fg, 1, tt_p)
            mk_p.append(k_hd.reshape(bp, tp, H_C, DH_C))
            mv_p.append(v_hd.reshape(bp, tp, H_C, DH_C))

            q, k, v, gate = _inproj(ys, shift_s, scale_s, g_l, w_cat, splits_s, dtypes, nb_s, ts)
            ck = cache_moba_k.reshape(-1, page * H_C, DH_C)
            cv = cache_moba_v.reshape(-1, page * H_C, DH_C)
            att = _moba_sample(q, k, v, ck, cv, page_table, o * n_pool)
            ys = _outproj(ys, [att], gate, gate_s, w_out, fg, nb_s, ts)
            mk_s.append(k.reshape(bs_, ts, H_C, DH_C))
            mv_s.append(v.reshape(bs_, ts, H_C, DH_C))
    return (yp, ys, jnp.stack(gla_p), jnp.stack(gla_s), jnp.stack(dk_p), jnp.stack(dv_p), jnp.stack(dk_s),
            jnp.stack(dv_s), jnp.stack(mk_p), jnp.stack(mv_p), jnp.stack(mk_s), jnp.stack(mv_s))
```

```python
import functools
import math

import jax
import jax.numpy as jnp
import numpy as np
from jax import lax
from jax.experimental import pallas as pl
from jax.experimental.pallas import tpu as pltpu

F32 = jnp.float32
BF16 = jnp.bfloat16

D_MODEL = 1024
H_A, DK_A, DV_A = 4, 64, 128
GLA_LOWRANK = 16
GLA_TAU = 16.0
GLA_CHUNK = 64
GLA_ATT_ROWS = 256
H_B, DK_B, DV_B = 4, 64, 128
H_C, DH_C = 8, 128
MOBA_BLOCK = 256
MOBA_TOPK = 3
W_A = H_A * DV_A
W_B = H_B * DV_B
W_C = H_C * DH_C
EPS = 1e-6
NEG = -1e30
LANE = 128
VT_ROWS = LANE + 16
KV_GROUP_KEYS = 512
FILL_TILE = 256
MOBA_Q_TILE = 1024
VMEM_LIMIT = 56 * 1024 * 1024

HIGHEST = lax.Precision.HIGHEST


def _bf16_pieces(x, n):
    out, r = [], float(x)
    for _ in range(n):
        p = float(np.asarray(r, np.float32).astype(jnp.bfloat16).astype(np.float32))
        out.append(p)
        r -= p
    return out


LOG2E = math.log2(math.e)
LOG2E_PIECES = _bf16_pieces(LOG2E, 3)


def _dot(a, b):
    return jnp.dot(a, b, preferred_element_type=F32)


def _dot_nt(a, b, precision=None):
    return lax.dot_general(a, b, (((1,), (1,)), ((), ())), preferred_element_type=F32, precision=precision)


def _dot_tn(a, b):
    return lax.dot_general(a, b, (((0,), (0,)), ((), ())), preferred_element_type=F32)


def _split_bf16(x, parts):
    out = []
    for _ in range(parts):
        p = x.astype(BF16)
        out.append(p)
        x = x - p.astype(F32)
    return out


def _dot_nt_split(a, b):
    ah, al = _split_bf16(a, 2)
    bh, bl = _split_bf16(b, 2)
    return _dot_nt(jnp.concatenate([ah, ah, al], axis=1), jnp.concatenate([bh, bl, bh], axis=1))


def _rms(x, g):
    return x * lax.rsqrt(jnp.mean(x * x, axis=-1, keepdims=True) + EPS) * g


def _silu(x):
    return x * jax.nn.sigmoid(x)


def _params(*sem):
    return pltpu.CompilerParams(dimension_semantics=sem, vmem_limit_bytes=VMEM_LIMIT)


def _ada_kernel(c_ref, w_ref, b_ref, o_ref):
    a = _silu(c_ref[...]).astype(BF16)
    o_ref[0] = _dot(a, w_ref[0].astype(BF16)) + b_ref[0]


def _ada(c_all, w_ada, b_ada):
    depth = w_ada.shape[0]
    mp = c_all.shape[0]
    tn = 1024
    return pl.pallas_call(
        _ada_kernel,
        out_shape=jax.ShapeDtypeStruct((depth, mp, 3 * D_MODEL), F32),
        grid=(depth, 3 * D_MODEL // tn),
        in_specs=[
            pl.BlockSpec((mp, D_MODEL), lambda l, j: (0, 0)),
            pl.BlockSpec((1, D_MODEL, tn), lambda l, j: (l, 0, j)),
            pl.BlockSpec((1, 1, tn), lambda l, j: (l, 0, j)),
        ],
        out_specs=pl.BlockSpec((1, mp, tn), lambda l, j: (l, 0, j)),
        compiler_params=_params("arbitrary", "arbitrary"),
        name="ada_mod",
    )(c_all, w_ada, b_ada.reshape(depth, 1, 3 * D_MODEL))


def _inproj_kernel(x_ref, shift_ref, scale_ref, g_ref, w_ref, *out_refs, splits):
    nb, tt, d = x_ref.shape
    x = x_ref[...]
    h = _rms(x, g_ref[...]) * (1.0 + scale_ref[...]) + shift_ref[...]
    h2 = h.reshape(nb * tt, d).astype(BF16)
    head_refs = list(out_refs[len(splits):])
    for (start, width, mult, n_heads), o_ref in zip(splits, out_refs):
        z = _dot(h2, w_ref[:, start:start + width])
        if mult != 1.0:
            z = z * mult
        o_ref[...] = z.reshape(nb, tt, width).astype(o_ref.dtype)
        if n_heads:
            hr = head_refs.pop(0)
            for hd in range(n_heads):
                hr.at[0][pl.ds(hd, tt, stride=n_heads), :] = z[:, hd * LANE:(hd + 1) * LANE]


def _inproj(x, shift, scale, g, w, splits, dtypes, nb, tt):
    nseq, t, d = x.shape
    n = w.shape[1]
    grid = (nseq // nb, t // tt)
    xmap = lambda i, j: (i, j, 0)
    out_shape = [jax.ShapeDtypeStruct((nseq, t, wd), dt) for (_, wd, _, _), dt in zip(splits, dtypes)]
    out_specs = [pl.BlockSpec((nb, tt, wd), xmap) for (_, wd, _, _) in splits]
    for (_, wd, _, nh) in splits:
        if nh:
            assert nb == 1 and wd == nh * LANE
            out_shape.append(jax.ShapeDtypeStruct((nseq, t * nh, LANE), F32))
            out_specs.append(pl.BlockSpec((1, tt * nh, LANE), xmap))
    return pl.pallas_call(
        functools.partial(_inproj_kernel, splits=tuple(splits)),
        out_shape=out_shape,
        grid=grid,
        in_specs=[
            pl.BlockSpec((nb, tt, d), xmap),
            pl.BlockSpec((nb, 1, d), lambda i, j: (i, 0, 0)),
            pl.BlockSpec((nb, 1, d), lambda i, j: (i, 0, 0)),
            pl.BlockSpec((1, 1, d), lambda i, j: (0, 0, 0)),
            pl.BlockSpec((d, n), lambda i, j: (0, 0)),
        ],
        out_specs=out_specs,
        compiler_params=_params("arbitrary", "arbitrary"),
        name="norm_inproj",
    )(x, shift, scale, g, w)


def _outproj_kernel(*refs, n_parts, final):
    x_ref = refs[0]
    o_refs = refs[1:1 + n_parts]
    gate_ref, ag_ref, w_ref = refs[1 + n_parts:4 + n_parts]
    rest = refs[4 + n_parts:]
    fg_ref = rest[0] if final else None
    y_ref = rest[-1]
    nb, tt, d = x_ref.shape
    rows = nb * tt
    sg = _silu(gate_ref[...].reshape(rows, -1))
    acc = jnp.zeros((rows, d), F32)
    col = 0
    for o_ref in o_refs:
        wd = o_ref.shape[-1]
        u = (o_ref[...].reshape(rows, wd) * sg[:, col:col + wd]).astype(BF16)
        acc = acc + _dot(u, w_ref[col:col + wd, :])
        col += wd
    y = x_ref[...] + ag_ref[...] * acc.reshape(nb, tt, d)
    if final:
        y = _rms(y, fg_ref[...])
    y_ref[...] = y


def _outproj(x, o_parts, gate, ada_gate, w, final_g, nb, tt):
    nseq, t, d = x.shape
    grid = (nseq // nb, t // tt)
    xmap = lambda i, j: (i, j, 0)
    final = final_g is not None
    in_specs = [pl.BlockSpec((nb, tt, d), xmap)]
    in_specs += [pl.BlockSpec((nb, tt, o.shape[-1]), xmap) for o in o_parts]
    in_specs += [
        pl.BlockSpec((nb, tt, gate.shape[-1]), xmap),
        pl.BlockSpec((nb, 1, d), lambda i, j: (i, 0, 0)),
        pl.BlockSpec(w.shape, lambda i, j: (0, 0)),
    ]
    args = [x, *o_parts, gate, ada_gate, w]
    if final:
        in_specs.append(pl.BlockSpec((1, 1, d), lambda i, j: (0, 0, 0)))
        args.append(final_g)
    return pl.pallas_call(
        functools.partial(_outproj_kernel, n_parts=len(o_parts), final=final),
        out_shape=jax.ShapeDtypeStruct((nseq, t, d), F32),
        grid=grid,
        in_specs=in_specs,
        out_specs=pl.BlockSpec((nb, tt, d), xmap),
        compiler_params=_params("arbitrary", "arbitrary"),
        name="gate_outproj",
    )(*args)


def _gla_rows(q, k, v, glow, wg, bg, gnorm, states, seq_len):
    r = q.shape[0]
    ns = r // seq_len
    la = jax.nn.log_sigmoid(_dot(glow.astype(BF16), wg) + bg) / GLA_TAU
    ri = lax.broadcasted_iota(jnp.int32, (r, r), 0)
    ci = lax.broadcasted_iota(jnp.int32, (r, r), 1)
    same = (ri // seq_len) == (ci // seq_len)
    causal = same & (ci <= ri)
    cum = jnp.dot(causal.astype(F32), la, preferred_element_type=F32, precision=HIGHEST)
    tot = jnp.dot(same.astype(F32), la, preferred_element_type=F32, precision=HIGHEST)
    q_all = q * (DK_A ** -0.5) * jnp.exp(cum)
    k_all = k * jnp.exp(-cum)
    ke_all = k * jnp.exp(tot - cum)
    dec = jnp.exp(tot)
    er = lax.broadcasted_iota(jnp.int32, (DK_A, DK_A), 0)
    ec = lax.broadcasted_iota(jnp.int32, (DK_A, DK_A), 1)
    eye = (er == ec).astype(F32)
    outs = []
    new_states = [[None] * H_A for _ in range(ns)]
    for h in range(H_A):
        ks = slice(h * DK_A, (h + 1) * DK_A)
        vs = slice(h * DV_A, (h + 1) * DV_A)
        q_in = q_all[:, ks]
        k_end = ke_all[:, ks]
        v_h = v[:, vs]
        att = jnp.where(causal, _dot_nt(q_in.astype(BF16), k_all[:, ks].astype(BF16)), 0.0)
        o = _dot(att.astype(BF16), v_h.astype(BF16))
        o_rows = []
        for j in range(ns):
            rs = slice(j * seq_len, (j + 1) * seq_len)
            s_old = states[j][h]
            o_rows.append(o[rs] + _dot(q_in[rs].astype(BF16), s_old.astype(BF16)))
            d_col = jnp.sum(eye * dec[j * seq_len:j * seq_len + 1, ks], axis=1, keepdims=True)
            new_states[j][h] = s_old * d_col + _dot_tn(k_end[rs].astype(BF16), v_h[rs].astype(BF16))
        o = o_rows[0] if ns == 1 else jnp.concatenate(o_rows, axis=0)
        outs.append(_rms(o, gnorm))
    return outs, new_states


def _gla_prompt_kernel(qkv_ref, glow_ref, wg_ref, bg_ref, gn_ref, o_ref, s_ref, st_sc):
    t = pl.program_id(1)
    tt = qkv_ref.shape[1]
    ch = GLA_CHUNK
    nc = tt // ch
    nq = H_A * DK_A

    @pl.when(t == 0)
    def _():
        st_sc[...] = jnp.zeros_like(st_sc)

    q = qkv_ref[0, :, 0:nq]
    k = qkv_ref[0, :, nq:2 * nq]
    la = jax.nn.log_sigmoid(_dot(glow_ref[0].astype(BF16), wg_ref[...]) + bg_ref[...]) / GLA_TAU
    la_w = jnp.concatenate([la[c * ch:(c + 1) * ch] for c in range(nc)], axis=1)
    ri = lax.broadcasted_iota(jnp.int32, (ch, ch), 0)
    ci = lax.broadcasted_iota(jnp.int32, (ch, ch), 1)
    tril = (ci <= ri).astype(BF16)
    cum_w = _dot(jnp.concatenate([tril] * 3, axis=1), jnp.concatenate(_split_bf16(la_w, 3), axis=0))
    tot_w = jnp.broadcast_to(cum_w[ch - 1:ch], cum_w.shape)
    tall = lambda a: jnp.concatenate([a[:, c * nq:(c + 1) * nq] for c in range(nc)], axis=0)
    cum = tall(cum_w)
    tot = tall(tot_w)
    q_all = q * (DK_A ** -0.5) * jnp.exp(cum)
    k_all = k * jnp.exp(-cum)
    ke_all = k * jnp.exp(tot - cum)
    dec = jnp.exp(tot)
    sub = min(GLA_ATT_ROWS, tt)
    rr = lax.broadcasted_iota(jnp.int32, (sub, sub), 0)
    cc = lax.broadcasted_iota(jnp.int32, (sub, sub), 1)
    causal = (rr // ch == cc // ch) & (cc <= rr)
    eye = (ri[:DK_A, :DK_A] == ci[:DK_A, :DK_A]).astype(F32)
    gn = gn_ref[...]
    for h in range(H_A):
        ks = slice(h * DK_A, (h + 1) * DK_A)
        q_in = q_all[:, ks].astype(BF16)
        k_end = ke_all[:, ks].astype(BF16)
        v_h = qkv_ref[0, :, 2 * nq + h * DV_A:2 * nq + (h + 1) * DV_A].astype(BF16)
        k_in = k_all[:, ks].astype(BF16)
        o_parts = []
        for s0 in range(0, tt, sub):
            ss = slice(s0, s0 + sub)
            att = jnp.where(causal, _dot_nt(q_in[ss], k_in[ss]), 0.0)
            o_parts.append(_dot(att.astype(BF16), v_h[ss]))
        o = jnp.concatenate(o_parts, axis=0)
        s_h = st_sc[h]
        rows_out = []
        for c in range(nc):
            rs = slice(c * ch, (c + 1) * ch)
            rows_out.append(o[rs] + _dot(q_in[rs], s_h.astype(BF16)))
            d_col = jnp.sum(eye * dec[c * ch:c * ch + 1, ks], axis=1, keepdims=True)
            s_h = s_h * d_col + _dot_tn(k_end[rs], v_h[rs])
        st_sc[h] = s_h
        o_ref[0, :, h * DV_A:(h + 1) * DV_A] = _rms(jnp.concatenate(rows_out, axis=0), gn)

    @pl.when(t == pl.num_programs(1) - 1)
    def _():
        s_ref[0] = st_sc[...]


def _gla_prompt(qkv, glow, wg, bg, gn, tt):
    b, t, _ = qkv.shape
    assert t % tt == 0 and tt % GLA_CHUNK == 0
    return pl.pallas_call(
        _gla_prompt_kernel,
        out_shape=[jax.ShapeDtypeStruct((b, t, W_A), F32), jax.ShapeDtypeStruct((b, H_A, DK_A, DV_A), F32)],
        grid=(b, t // tt),
        in_specs=[
            pl.BlockSpec((1, tt, qkv.shape[-1]), lambda i, j: (i, j, 0)),
            pl.BlockSpec((1, tt, LANE), lambda i, j: (i, j, 0)),
            pl.BlockSpec(wg.shape, lambda i, j: (0, 0)),
            pl.BlockSpec(bg.shape, lambda i, j: (0, 0)),
            pl.BlockSpec(gn.shape, lambda i, j: (0, 0)),
        ],
        out_specs=[
            pl.BlockSpec((1, tt, W_A), lambda i, j: (i, j, 0)),
            pl.BlockSpec((1, H_A, DK_A, DV_A), lambda i, j: (i, 0, 0, 0)),
        ],
        scratch_shapes=[pltpu.VMEM((H_A, DK_A, DV_A), F32)],
        compiler_params=_params("arbitrary", "arbitrary"),
        name="gla_prompt",
    )(qkv, glow, wg, bg, gn)


def _gla_sample_kernel(qkv_ref, glow_ref, s0_ref, wg_ref, bg_ref, gn_ref, o_ref, s_ref):
    nb, ts, _ = qkv_ref.shape
    nq = H_A * DK_A
    r = nb * ts
    qkv = qkv_ref[...].reshape(r, qkv_ref.shape[-1])
    glow = glow_ref[...].reshape(r, LANE)
    states = [[s0_ref[j, h] for h in range(H_A)] for j in range(nb)]
    outs, new_states = _gla_rows(qkv[:, 0:nq], qkv[:, nq:2 * nq], qkv[:, 2 * nq:2 * nq + W_A], glow,
                                 wg_ref[...], bg_ref[...], gn_ref[...], states, ts)
    for h in range(H_A):
        o_ref[:, :, h * DV_A:(h + 1) * DV_A] = outs[h].reshape(nb, ts, DV_A)
        for j in range(nb):
            s_ref[j, h] = new_states[j][h]


def _gla_sample(qkv, glow, s0, wg, bg, gn, nb):
    b, ts, _ = qkv.shape
    assert b % nb == 0 and ts % 8 == 0 and ts <= GLA_CHUNK
    return pl.pallas_call(
        _gla_sample_kernel,
        out_shape=[jax.ShapeDtypeStruct((b, ts, W_A), F32), jax.ShapeDtypeStruct((b, H_A, DK_A, DV_A), F32)],
        grid=(b // nb,),
        in_specs=[
            pl.BlockSpec((nb, ts, qkv.shape[-1]), lambda i: (i, 0, 0)),
            pl.BlockSpec((nb, ts, LANE), lambda i: (i, 0, 0)),
            pl.BlockSpec((nb, H_A, DK_A, DV_A), lambda i: (i, 0, 0, 0)),
            pl.BlockSpec(wg.shape, lambda i: (0, 0)),
            pl.BlockSpec(bg.shape, lambda i: (0, 0)),
            pl.BlockSpec(gn.shape, lambda i: (0, 0)),
        ],
        out_specs=[
            pl.BlockSpec((nb, ts, W_A), lambda i: (i, 0, 0)),
            pl.BlockSpec((nb, H_A, DK_A, DV_A), lambda i: (i, 0, 0, 0)),
        ],
        compiler_params=_params("arbitrary"),
        name="gla_sample",
    )(qkv, glow, s0, wg, bg, gn)


def _attend_cols(kaug_sc, vt_sc, q_aug, i, tq, acc_sc):
    ncol = q_aug.shape[0]
    t = kaug_sc.shape[0]
    gk = min(max(KV_GROUP_KEYS, tq), t)
    assert gk % tq == 0 and t % gk == 0
    kr = lax.broadcasted_iota(jnp.int32, (gk, ncol), 0)
    qc = lax.broadcasted_iota(jnp.int32, (gk, ncol), 1) % tq
    for v in range(t // gk):
        head = v * gk

        @pl.when((i * tq) // gk == v)
        def _(head=head):
            s_t = _dot_nt(kaug_sc[head:head + gk, :], q_aug)
            s_t = jnp.where(kr + head <= qc + i * tq, s_t, NEG)
            m = jnp.max(s_t, axis=0, keepdims=True)
            if head > 0:
                s_h = _dot_nt(kaug_sc[0:head, :], q_aug)
                m = jnp.maximum(m, jnp.max(s_h, axis=0, keepdims=True))
            acc = _dot(vt_sc[:, head:head + gk], jnp.exp2(s_t - m).astype(BF16))
            if head > 0:
                acc = acc + _dot(vt_sc[:, 0:head], jnp.exp2(s_h - m).astype(BF16))
            acc_sc[...] = acc


def _log2e_coef(idx, base):
    r = idx - base
    out = jnp.zeros(idx.shape, F32)
    for n, piece in enumerate(LOG2E_PIECES):
        out = jnp.where((r == n) | (r == n + 3), piece, out)
    return out


def _fill_kv(k_ref, v_ref, kaug_sc, vt_sc, tk, aug_fn, km_sc=None):
    nblk = kaug_sc.shape[0] // tk
    dv = v_ref.shape[-1]
    rowv = lax.broadcasted_iota(jnp.int32, (VT_ROWS - dv, tk), 0)
    ones_rows = jnp.where(rowv == 0, 1.0, 0.0).astype(BF16)
    for n in range(nblk):
        rows = slice(n * tk, (n + 1) * tk)
        kt = k_ref[0, rows, :]
        if km_sc is not None:
            km_sc[n:n + 1, :] = jnp.mean(kt, axis=0, keepdims=True)
        kaug_sc[rows, 0:LANE] = kt.astype(BF16)
        kaug_sc[rows, LANE:] = aug_fn(n).astype(BF16)
        vt_sc[0:dv, rows] = v_ref[0, rows, :].T.astype(BF16)
        vt_sc[dv:, rows] = ones_rows


def _lambda(l_ref, lam_init):
    lv = l_ref[...]
    s1 = jnp.sum(lv[0:1] * lv[1:2], axis=1, keepdims=True)
    s2 = jnp.sum(lv[2:3] * lv[3:4], axis=1, keepdims=True)
    return jnp.exp(s1) - jnp.exp(s2) + lam_init


def _diff_prompt_kernel(slope_ref, q_ref, k_ref, v_ref, l_ref, gn_ref, o_ref, kaug_sc, vt_sc, acc_sc, *, lam_init):
    h = pl.program_id(1)
    i = pl.program_id(2)
    tq = q_ref.shape[1]

    @pl.when(i == 0)
    def _():
        slope = slope_ref[h]
        ft = min(FILL_TILE, tq)
        c = lax.broadcasted_iota(jnp.int32, (ft, LANE), 0).astype(F32)
        col = lax.broadcasted_iota(jnp.int32, (ft, LANE), 1)

        def aug(n):
            return jnp.where(col < 3, slope * float(ft * n), jnp.where(col < 6, slope * c, 0.0))

        _fill_kv(k_ref, v_ref, kaug_sc, vt_sc, ft, aug)

    q = q_ref[0]
    lane = lax.broadcasted_iota(jnp.int32, q.shape, 1)
    zero = jnp.zeros_like(q)
    lcoef = _log2e_coef(lane, 0).astype(BF16)
    q_aug = jnp.concatenate([
        jnp.concatenate([jnp.where(lane < DK_B, q, zero), lcoef], axis=1),
        jnp.concatenate([jnp.where(lane >= DK_B, q, zero), lcoef], axis=1)], axis=0)
    _attend_cols(kaug_sc, vt_sc, q_aug, i, tq, acc_sc)
    acc = acc_sc[...]
    o_t = acc[0:DV_B] / acc[DV_B:DV_B + 1]
    lam = _lambda(l_ref, lam_init)
    o = (o_t[:, :tq] - lam * o_t[:, tq:]).T
    o_ref[0] = _rms(o, gn_ref[...]) * (1.0 - lam_init)


def _diff_prompt(q, k, v, slopes, lvec, gn, lam_init, tq):
    b, t, _ = q.shape
    assert t % tq == 0
    nblk = t // tq
    kv_spec = pl.BlockSpec((1, t, LANE), lambda bi, h, i, s: (bi, 0, h))
    return pl.pallas_call(
        functools.partial(_diff_prompt_kernel, lam_init=lam_init),
        out_shape=jax.ShapeDtypeStruct((b, t, W_B), F32),
        grid_spec=pltpu.PrefetchScalarGridSpec(
            num_scalar_prefetch=1,
            grid=(b, H_B, nblk),
            in_specs=[
                pl.BlockSpec((1, tq, LANE), lambda bi, h, i, s: (bi, i, h)),
                kv_spec,
                kv_spec,
                pl.BlockSpec(lvec.shape, lambda bi, h, i, s: (0, 0)),
                pl.BlockSpec(gn.shape, lambda bi, h, i, s: (0, 0)),
            ],
            out_specs=pl.BlockSpec((1, tq, LANE), lambda bi, h, i, s: (bi, i, h)),
            scratch_shapes=[
                pltpu.VMEM((t, 2 * LANE), BF16),
                pltpu.VMEM((VT_ROWS, t), BF16),
                pltpu.VMEM((VT_ROWS, 2 * tq), F32),
            ],
        ),
        compiler_params=_params("arbitrary", "arbitrary", "arbitrary"),
        name="diff_prompt",
    )(slopes, q, k, v, lvec, gn)


def _topk_mask(g, valid, n_axis):
    nb = g.shape[n_axis]
    g = jnp.where(valid, g, -jnp.inf)
    idx = lax.broadcasted_iota(jnp.int32, g.shape, n_axis)
    rank = jnp.zeros(g.shape, jnp.int32)
    for m in range(nb):
        gm = lax.slice_in_dim(g, m, m + 1, axis=n_axis)
        beats = (gm > g) | ((gm == g) & (m < idx))
        rank = rank + beats.astype(jnp.int32)
    return (rank < MOBA_TOPK) & valid


def _moba_prompt_kernel(slope_ref, q_ref, k_ref, v_ref, o_ref, km_sc, kaug_sc, vt_sc, acc_sc):
    h = pl.program_id(1)
    i = pl.program_id(2)
    tq = q_ref.shape[1]
    bs = MOBA_BLOCK
    nblk = km_sc.shape[0]

    @pl.when(i == 0)
    def _():
        slope = slope_ref[h]
        c = lax.broadcasted_iota(jnp.int32, (bs, LANE), 0).astype(F32)
        col = lax.broadcasted_iota(jnp.int32, (bs, LANE), 1)

        def aug(n):
            return jnp.where(col == n, 1.0, jnp.where(col < nblk, 0.0, jnp.where(
                col < nblk + 3, slope * float(bs * n), jnp.where(col < nblk + 6, slope * c, 0.0))))

        _fill_kv(k_ref, v_ref, kaug_sc, vt_sc, bs, aug, km_sc)

    q = q_ref[0]
    g = _dot_nt_split(km_sc[...], q)
    blk = lax.broadcasted_iota(jnp.int32, g.shape, 0)
    own = i * (tq // bs) + lax.broadcasted_iota(jnp.int32, g.shape, 1) // bs
    sel = _topk_mask(g, blk < own, 0)
    selb = jnp.where(sel | (blk == own), 0.0, NEG)
    row = lax.broadcasted_iota(jnp.int32, (LANE, tq), 0)
    coef = jnp.concatenate([selb, jnp.zeros((LANE - nblk, tq), F32)], axis=0)
    coef = coef + _log2e_coef(row, nblk)
    q_aug = jnp.concatenate([q * (DH_C ** -0.5 * LOG2E), coef.T], axis=1).astype(BF16)
    _attend_cols(kaug_sc, vt_sc, q_aug, i, tq, acc_sc)
    acc = acc_sc[...]
    o_ref[0] = (acc[0:DH_C] / acc[DH_C:DH_C + 1]).T


def _moba_prompt(q, k, v, slopes):
    b, t, _ = q.shape
    bs = MOBA_BLOCK
    tq = min(MOBA_Q_TILE, t)
    assert t % tq == 0 and tq % bs == 0
    nblk = t // bs
    assert nblk + 6 <= LANE
    kv_spec = pl.BlockSpec((1, t, LANE), lambda bi, h, i, s: (bi, 0, h))
    q_spec = pl.BlockSpec((1, tq, LANE), lambda bi, h, i, s: (bi, i, h))
    return pl.pallas_call(
        _moba_prompt_kernel,
        out_shape=jax.ShapeDtypeStruct((b, t, W_C), F32),
        grid_spec=pltpu.PrefetchScalarGridSpec(
            num_scalar_prefetch=1,
            grid=(b, H_C, t // tq),
            in_specs=[q_spec, kv_spec, kv_spec],
            out_specs=q_spec,
            scratch_shapes=[
                pltpu.VMEM((nblk, DH_C), F32),
                pltpu.VMEM((t, 2 * LANE), BF16),
                pltpu.VMEM((VT_ROWS, t), BF16),
                pltpu.VMEM((VT_ROWS, tq), F32),
            ],
        ),
        compiler_params=_params("arbitrary", "arbitrary", "arbitrary"),
        name="moba_prompt",
    )(slopes, q, k, v)


def _score_cols_mask(n_heads, n_maps, dk, ts):
    c = np.arange(LANE)[:, None]
    f = np.arange(n_heads * n_maps * dk)[None, :]
    ms = LANE // n_maps
    return (((c % ms) // ts == f // (n_maps * dk)) & (c // ms == (f % (n_maps * dk)) // dk)).astype(np.float32)


def _score_bias(n_heads, n_maps, ts, past_len, lpad):
    j = np.arange(lpad)[:, None]
    c = np.arange(LANE)[None, :]
    hh = (c % (LANE // n_maps)) // ts
    rel = past_len + c % ts - j
    slope = np.array([2.0 ** (-8.0 * (h + 1) / n_heads) for h in range(LANE // ts + 1)], np.float32)[hh]
    bias = np.where(hh < n_heads, -slope * rel.astype(np.float32), np.float32(0.0))
    return np.where((rel >= 0) & (j < past_len + ts), bias, np.float32(NEG)).astype(np.float32)


def _head_pair(p_ref, j, n_heads, page):
    rows = p_ref.at[0]
    return jnp.concatenate([rows[pl.ds(2 * j, page, stride=n_heads), :],
                            rows[pl.ds(2 * j + 1, page, stride=n_heads), :]], axis=1).astype(BF16)


def _page_scores(q, wmask_ref, kp_refs, kn_ref, s_sc, past_len, n_heads):
    ts, width = q.shape
    ncol = s_sc.shape[1]
    page = kp_refs[0].shape[1] // n_heads
    wt = (jnp.broadcast_to(q[None], (ncol // ts, ts, width)).reshape(ncol, width) * wmask_ref[...]).astype(BF16)
    for g, kp_ref in enumerate(kp_refs):
        acc = None
        for j in range(n_heads // 2):
            part = _dot_nt(_head_pair(kp_ref, j, n_heads, page), wt[:, 2 * j * LANE:(2 * j + 2) * LANE])
            acc = part if acc is None else acc + part
        s_sc[g * page:(g + 1) * page, :] = acc
    s_sc[past_len:past_len + ts, :] = _dot_nt(kn_ref[0].astype(BF16), wt)
    s_sc[past_len + ts:, :] = jnp.zeros((s_sc.shape[0] - past_len - ts, ncol), F32)


def _page_values(a_t, vp_refs, vn_ref, past_len, n_heads):
    page = vp_refs[0].shape[1] // n_heads
    ts, width = vn_ref.shape[1], vn_ref.shape[2]
    acc = [jnp.zeros((2 * ts, 2 * LANE), F32) for _ in range(n_heads // 2)]
    for g, vp_ref in enumerate(vp_refs):
        at = a_t[g * page:(g + 1) * page].T
        for j in range(n_heads // 2):
            acc[j] = acc[j] + _dot(at[2 * ts * j:2 * ts * (j + 1)].astype(BF16), _head_pair(vp_ref, j, n_heads, page))
    v_tail = jnp.concatenate([vn_ref[0], jnp.zeros((LANE - ts, width), F32)], axis=0)
    o_tail = _dot(a_t[past_len:].T.astype(BF16), v_tail.astype(BF16))
    outs = []
    for h in range(n_heads):
        j, r = divmod(h, 2)
        outs.append(acc[j][r * ts:(r + 1) * ts, r * LANE:(r + 1) * LANE]
                    + o_tail[h * ts:(h + 1) * ts, h * LANE:(h + 1) * LANE])
    return outs


def _softmax_rows(s):
    e = jnp.exp(s - jnp.max(s, axis=0, keepdims=True))
    return e * (1.0 / jnp.sum(e, axis=0, keepdims=True))


def _diff_sample_kernel(pt_ref, q_ref, kn_ref, vn_ref, wmask_ref, bias_ref, l_ref, gn_ref, *rest,
                        n_pages, lam_init, past_len):
    kp_refs, vp_refs = rest[:n_pages], rest[n_pages:2 * n_pages]
    o_ref, s_sc = rest[2 * n_pages:]
    ts = q_ref.shape[1]
    _page_scores(q_ref[0], wmask_ref, kp_refs, kn_ref, s_sc, past_len, H_B)
    pr = _softmax_rows(s_sc[...] + bias_ref[...])
    lam = _lambda(l_ref, lam_init)
    a_t = pr - lam * pltpu.roll(pr, LANE // 2, axis=1)
    gn = gn_ref[...]
    for h, o in enumerate(_page_values(a_t, vp_refs, vn_ref, past_len, H_B)):
        o_ref[0, :, h * DV_B:(h + 1) * DV_B] = _rms(o, gn) * (1.0 - lam_init)


def _moba_sample_kernel(pt_ref, q_ref, kn_ref, vn_ref, wmask_ref, bias_ref, *rest, n_pages, past_len):
    kp_refs, vp_refs = rest[:n_pages], rest[n_pages:2 * n_pages]
    o_ref, s_sc = rest[2 * n_pages:]
    ts = q_ref.shape[1]
    bs = MOBA_BLOCK
    nbp = past_len // bs
    _page_scores(q_ref[0] * (DH_C ** -0.5), wmask_ref, kp_refs, kn_ref, s_sc, past_len, H_C)
    s = s_sc[...]
    blocks = [s[n * bs:(n + 1) * bs] for n in range(nbp)]
    g = jnp.concatenate([jnp.sum(b, axis=0, keepdims=True) for b in blocks], axis=0)
    selb = jnp.where(_topk_mask(g, jnp.full(g.shape, True), 0), 0.0, NEG)
    bias = bias_ref[...]
    parts = [blocks[n] + bias[n * bs:(n + 1) * bs] + selb[n:n + 1] for n in range(nbp)]
    parts.append(s[past_len:] + bias[past_len:])
    pr = _softmax_rows(jnp.concatenate(parts, axis=0))
    for h, o in enumerate(_page_values(pr, vp_refs, vn_ref, past_len, H_C)):
        o_ref[0, :, h * DH_C:(h + 1) * DH_C] = o


def _paged_call(body, name, q, k_new, v_new, cache_k, cache_v, page_table, page_base, n_heads, n_maps, extra):
    b, ts, width = q.shape
    n_pages = page_table.shape[1]
    page = cache_k.shape[1] // n_heads
    vwidth = v_new.shape[2]
    past_len = n_pages * page
    assert n_maps * n_heads * ts <= LANE and ts % 8 == 0 and n_heads % 2 == 0
    assert width == n_heads * LANE and vwidth == n_heads * LANE and cache_k.shape[2] == LANE
    ncol = LANE
    lpad = past_len + LANE
    wmask = jnp.asarray(_score_cols_mask(n_heads, n_maps, width // (n_heads * n_maps), ts))
    bias = jnp.asarray(_score_bias(n_heads, n_maps, ts, past_len, lpad))
    const = lambda a: pl.BlockSpec(a.shape, lambda bi, pt: (0,) * a.ndim)
    row = lambda w: pl.BlockSpec((1, ts, w), lambda bi, pt: (bi, 0, 0))
    page_spec = lambda g: pl.BlockSpec((1, page * n_heads, LANE),
                                       lambda bi, pt: (page_base + pt[bi * n_pages + g], 0, 0))
    in_specs = [row(width), row(width), row(vwidth), const(wmask), const(bias)] + [const(a) for a in extra]
    in_specs += [page_spec(g) for g in range(n_pages)] * 2
    return pl.pallas_call(
        functools.partial(body, n_pages=n_pages, past_len=past_len),
        out_shape=jax.ShapeDtypeStruct((b, ts, vwidth), F32),
        grid_spec=pltpu.PrefetchScalarGridSpec(
            num_scalar_prefetch=1,
            grid=(b,),
            in_specs=in_specs,
            out_specs=row(vwidth),
            scratch_shapes=[pltpu.VMEM((lpad, ncol), F32)],
        ),
        compiler_params=_params("arbitrary"),
        name=name,
    )(page_table.reshape(-1), q, k_new, v_new, wmask, bias, *extra, *([cache_k] * n_pages), *([cache_v] * n_pages))


def _diff_sample(q, k_new, v_new, cache_k, cache_v, page_table, page_base, lvec, gn, lam_init):
    body = functools.partial(_diff_sample_kernel, lam_init=lam_init)
    return _paged_call(body, "diff_sample", q, k_new, v_new, cache_k, cache_v, page_table, page_base, H_B, 2,
                       [lvec, gn])


def _moba_sample(q, k_new, v_new, cache_k, cache_v, page_table, page_base):
    past_len = page_table.shape[1] * cache_k.shape[1] // H_C
    assert past_len % MOBA_BLOCK == 0 and past_len // MOBA_BLOCK >= MOBA_TOPK and q.shape[1] <= MOBA_BLOCK
    return _paged_call(_moba_sample_kernel, "moba_sample", q, k_new, v_new, cache_k, cache_v, page_table, page_base,
                       H_C, 1, [])


def _alibi_slopes(n):
    return jnp.array([2.0 ** (-8.0 * (h + 1) / n) for h in range(n)], dtype=F32)


def kernel(x_prompt, x_sample, c_prompt, c_sample, state_gla, cache_diff_k, cache_diff_v, cache_moba_k, cache_moba_v,
           page_table, norm_g, w_ada, b_ada, w_in_e, w_gla_gate, b_gla_gate, g_gla_norm, lam_q1, lam_k1, lam_q2,
           lam_k2, g_diff_norm, w_out_e, w_in_o, w_out_o, final_g):
    bp, tp, d = x_prompt.shape
    bs_, ts, _ = x_sample.shape
    depth = norm_g.shape[0]
    n_pool, page = cache_diff_k.shape[1], cache_diff_k.shape[2]
    assert d == D_MODEL

    tt_p = min(512, tp)
    nb_s = min(64, bs_)
    tq_diff = min(512, tp)

    mrows = bp + bs_
    mpad = -(-mrows // 8) * 8
    c_all = jnp.concatenate([c_prompt, c_sample, jnp.zeros((mpad - mrows, d), F32)], axis=0)
    mod = _ada(c_all, w_ada, b_ada)

    def mod_parts(l, lo, hi):
        m = mod[l, lo:hi]
        return m[:, None, 0:d], m[:, None, d:2 * d], m[:, None, 2 * d:3 * d]

    slopes_b = _alibi_slopes(H_B)
    slopes_c = _alibi_slopes(H_C)
    nqa = H_A * DK_A
    yp, ys = x_prompt, x_sample
    gla_p, gla_s, dk_p, dv_p, dk_s, dv_s, mk_p, mv_p, mk_s, mv_s = ([] for _ in range(10))
    for l in range(depth):
        shift_p, scale_p, gate_p = mod_parts(l, 0, bp)
        shift_s, scale_s, gate_s = mod_parts(l, bp, bp + bs_)
        g_l = norm_g[l].reshape(1, 1, d)
        last = l == depth - 1
        fg = final_g.reshape(1, 1, d) if last else None
        if l % 2 == 0:
            e = l // 2
            w = w_in_e[e]
            c0 = 2 * nqa + W_A
            c1 = c0 + GLA_LOWRANK
            w_cat = jnp.concatenate([w[:, :c0], w[:, c0:c1], jnp.zeros((d, LANE - GLA_LOWRANK), F32), w[:, c1:]],
                                    axis=1).astype(BF16)
            o0 = c0 + LANE
            qmul = DK_B ** -0.5
            splits_s = [(0, c0, 1.0, 0), (c0, LANE, 1.0, 0), (o0, 2 * H_B * DK_B, qmul, 0),
                        (o0 + 512, 512, 1.0, 0), (o0 + 1024, 512, 1.0, 0), (o0 + 1536, W_A + W_B, 1.0, 0)]
            splits = [sp[:3] + (nh,) for sp, nh in zip(splits_s, (0, 0, 0, H_B, H_B, 0))]
            splits[2] = splits[2][:2] + (qmul * LOG2E, 0)
            dtypes = [F32, F32, BF16, F32, F32, F32]
            wg = jnp.concatenate([w_gla_gate[e], jnp.zeros((LANE - GLA_LOWRANK, nqa), F32)], axis=0).astype(BF16)
            bg = b_gla_gate[e].reshape(1, nqa)
            gn_a = g_gla_norm[e].reshape(1, DV_A)
            gn_b = g_diff_norm[e].reshape(1, DV_B)
            lvec = jnp.stack([lam_q1[e], lam_k1[e], lam_q2[e], lam_k2[e]])
            lam_init = 0.8 - 0.6 * math.exp(-0.3 * l)
            w_out = w_out_e[e].astype(BF16)

            qkv_a, glow, q_b, k_b, v_b, gate, k_hd, v_hd = _inproj(yp, shift_p, scale_p, g_l, w_cat, splits, dtypes,
                                                                   1, tt_p)
            o_a, s_new = _gla_prompt(qkv_a, glow, wg, bg, gn_a, tt_p)
            o_b = _diff_prompt(q_b, k_b, v_b, slopes_b, lvec, gn_b, lam_init, tq_diff)
            yp = _outproj(yp, [o_a, o_b], gate, gate_p, w_out, fg, 1, tt_p)
            gla_p.append(s_new)
            dk_p.append(k_hd.reshape(bp, tp, H_B, 2 * DK_B))
            dv_p.append(v_hd.reshape(bp, tp, H_B, DV_B))

            dtypes_s = [F32] * len(dtypes)
            qkv_a, glow, q_b, k_b, v_b, gate = _inproj(ys, shift_s, scale_s, g_l, w_cat, splits_s, dtypes_s, nb_s, ts)
            o_a, s_new = _gla_sample(qkv_a, glow, state_gla[e], wg, bg, gn_a, min(8, bs_))
            ck = cache_diff_k.reshape(-1, page * H_B, 2 * DK_B)
            cv = cache_diff_v.reshape(-1, page * H_B, DV_B)
            o_b = _diff_sample(q_b, k_b, v_b, ck, cv, page_table, e * n_pool, lvec, gn_b, lam_init)
            ys = _outproj(ys, [o_a, o_b], gate, gate_s, w_out, fg, nb_s, ts)
            gla_s.append(s_new)
            dk_s.append(k_b.reshape(bs_, ts, H_B, 2 * DK_B))
            dv_s.append(v_b.reshape(bs_, ts, H_B, DV_B))
        else:
            o = l // 2
            w_cat = w_in_o[o].astype(BF16)
            splits = [(0, W_C, 1.0, 0), (W_C, W_C, 1.0, H_C), (2 * W_C, W_C, 1.0, H_C), (3 * W_C, W_C, 1.0, 0)]
            splits_s = [sp[:3] + (0,) for sp in splits]
            dtypes = [F32, F32, F32, F32]
            w_out = w_out_o[o].astype(BF16)

            q, k, v, gate, k_hd, v_hd = _inproj(yp, shift_p, scale_p, g_l, w_cat, splits, dtypes, 1, tt_p)
            att = _moba_prompt(q, k, v, slopes_c)
            yp = _outproj(yp, [att], gate, gate_p, w_out, fg, 1, tt_p)
            mk_p.append(k_hd.reshape(bp, tp, H_C, DH_C))
            mv_p.append(v_hd.reshape(bp, tp, H_C, DH_C))

            q, k, v, gate = _inproj(ys, shift_s, scale_s, g_l, w_cat, splits_s, dtypes, nb_s, ts)
            ck = cache_moba_k.reshape(-1, page * H_C, DH_C)
            cv = cache_moba_v.reshape(-1, page * H_C, DH_C)
            att = _moba_sample(q, k, v, ck, cv, page_table, o * n_pool)
            ys = _outproj(ys, [att], gate, gate_s, w_out, fg, nb_s, ts)
            mk_s.append(k.reshape(bs_, ts, H_C, DH_C))
            mv_s.append(v.reshape(bs_, ts, H_C, DH_C))
    return (yp, ys, jnp.stack(gla_p), jnp.stack(gla_s), jnp.stack(dk_p), jnp.stack(dv_p), jnp.stack(dk_s),
            jnp.stack(dv_s), jnp.stack(mk_p), jnp.stack(mv_p), jnp.stack(mk_s), jnp.stack(mv_s))
```

```python
import functools
import math

import jax
import jax.numpy as jnp
import numpy as np
from jax import lax
from jax.experimental import pallas as pl
from jax.experimental.pallas import tpu as pltpu

F32 = jnp.float32
BF16 = jnp.bfloat16

D_MODEL = 1024
H_A, DK_A, DV_A = 4, 64, 128
GLA_LOWRANK = 16
GLA_TAU = 16.0
GLA_CHUNK = 64
GLA_ATT_ROWS = 256
H_B, DK_B, DV_B = 4, 64, 128
H_C, DH_C = 8, 128
MOBA_BLOCK = 256
MOBA_TOPK = 3
W_A = H_A * DV_A
W_B = H_B * DV_B
W_C = H_C * DH_C
EPS = 1e-6
NEG = -1e30
LANE = 128
VT_ROWS = LANE + 16
KV_GROUP_KEYS = 512
FILL_TILE = 256
MOBA_Q_TILE = 1024
VMEM_LIMIT = 56 * 1024 * 1024

HIGHEST = lax.Precision.HIGHEST


def _bf16_pieces(x, n):
    out, r = [], float(x)
    for _ in range(n):
        p = float(np.asarray(r, np.float32).astype(jnp.bfloat16).astype(np.float32))
        out.append(p)
        r -= p
    return out


LOG2E = math.log2(math.e)
LOG2E_PIECES = _bf16_pieces(LOG2E, 3)


def _dot(a, b):
    return jnp.dot(a, b, preferred_element_type=F32)


def _dot_nt(a, b, precision=None):
    return lax.dot_general(a, b, (((1,), (1,)), ((), ())), preferred_element_type=F32, precision=precision)


def _dot_tn(a, b):
    return lax.dot_general(a, b, (((0,), (0,)), ((), ())), preferred_element_type=F32)


def _split_bf16(x, parts):
    out = []
    for _ in range(parts):
        p = x.astype(BF16)
        out.append(p)
        x = x - p.astype(F32)
    return out


def _dot_nt_split(a, b):
    ah, al = _split_bf16(a, 2)
    bh, bl = _split_bf16(b, 2)
    return _dot_nt(jnp.concatenate([ah, ah, al], axis=1), jnp.concatenate([bh, bl, bh], axis=1))


def _rms(x, g):
    return x * lax.rsqrt(jnp.mean(x * x, axis=-1, keepdims=True) + EPS) * g


def _silu(x):
    return x * jax.nn.sigmoid(x)


def _params(*sem):
    return pltpu.CompilerParams(dimension_semantics=sem, vmem_limit_bytes=VMEM_LIMIT)


def _ada_kernel(c_ref, w_ref, b_ref, o_ref):
    a = _silu(c_ref[...]).astype(BF16)
    o_ref[0] = _dot(a, w_ref[0].astype(BF16)) + b_ref[0]


def _ada(c_all, w_ada, b_ada):
    depth = w_ada.shape[0]
    mp = c_all.shape[0]
    tn = 1024
    return pl.pallas_call(
        _ada_kernel,
        out_shape=jax.ShapeDtypeStruct((depth, mp, 3 * D_MODEL), F32),
        grid=(depth, 3 * D_MODEL // tn),
        in_specs=[
            pl.BlockSpec((mp, D_MODEL), lambda l, j: (0, 0)),
            pl.BlockSpec((1, D_MODEL, tn), lambda l, j: (l, 0, j)),
            pl.BlockSpec((1, 1, tn), lambda l, j: (l, 0, j)),
        ],
        out_specs=pl.BlockSpec((1, mp, tn), lambda l, j: (l, 0, j)),
        compiler_params=_params("arbitrary", "arbitrary"),
        name="ada_mod",
    )(c_all, w_ada, b_ada.reshape(depth, 1, 3 * D_MODEL))


def _inproj_kernel(x_ref, shift_ref, scale_ref, g_ref, w_ref, *out_refs, splits):
    nb, tt, d = x_ref.shape
    x = x_ref[...]
    h = _rms(x, g_ref[...]) * (1.0 + scale_ref[...]) + shift_ref[...]
    h2 = h.reshape(nb * tt, d).astype(BF16)
    head_refs = list(out_refs[len(splits):])
    for (start, width, mult, n_heads), o_ref in zip(splits, out_refs):
        z = _dot(h2, w_ref[:, start:start + width])
        if mult != 1.0:
            z = z * mult
        o_ref[...] = z.reshape(nb, tt, width).astype(o_ref.dtype)
        if n_heads:
            hr = head_refs.pop(0)
            for hd in range(n_heads):
                hr.at[0][pl.ds(hd, tt, stride=n_heads), :] = z[:, hd * LANE:(hd + 1) * LANE]


def _inproj(x, shift, scale, g, w, splits, dtypes, nb, tt):
    nseq, t, d = x.shape
    n = w.shape[1]
    grid = (nseq // nb, t // tt)
    xmap = lambda i, j: (i, j, 0)
    out_shape = [jax.ShapeDtypeStruct((nseq, t, wd), dt) for (_, wd, _, _), dt in zip(splits, dtypes)]
    out_specs = [pl.BlockSpec((nb, tt, wd), xmap) for (_, wd, _, _) in splits]
    for (_, wd, _, nh) in splits:
        if nh:
            assert nb == 1 and wd == nh * LANE
            out_shape.append(jax.ShapeDtypeStruct((nseq, t * nh, LANE), F32))
            out_specs.append(pl.BlockSpec((1, tt * nh, LANE), xmap))
    return pl.pallas_call(
        functools.partial(_inproj_kernel, splits=tuple(splits)),
        out_shape=out_shape,
        grid=grid,
        in_specs=[
            pl.BlockSpec((nb, tt, d), xmap),
            pl.BlockSpec((nb, 1, d), lambda i, j: (i, 0, 0)),
            pl.BlockSpec((nb, 1, d), lambda i, j: (i, 0, 0)),
            pl.BlockSpec((1, 1, d), lambda i, j: (0, 0, 0)),
            pl.BlockSpec((d, n), lambda i, j: (0, 0)),
        ],
        out_specs=out_specs,
        compiler_params=_params("arbitrary", "arbitrary"),
        name="norm_inproj",
    )(x, shift, scale, g, w)


def _outproj_kernel(*refs, n_parts, final):
    x_ref = refs[0]
    o_refs = refs[1:1 + n_parts]
    gate_ref, ag_ref, w_ref = refs[1 + n_parts:4 + n_parts]
    rest = refs[4 + n_parts:]
    fg_ref = rest[0] if final else None
    y_ref = rest[-1]
    nb, tt, d = x_ref.shape
    rows = nb * tt
    sg = _silu(gate_ref[...].astype(F32).reshape(rows, -1))
    acc = jnp.zeros((rows, d), F32)
    col = 0
    for o_ref in o_refs:
        wd = o_ref.shape[-1]
        u = (o_ref[...].astype(F32).reshape(rows, wd) * sg[:, col:col + wd]).astype(BF16)
        acc = acc + _dot(u, w_ref[col:col + wd, :])
        col += wd
    y = x_ref[...] + ag_ref[...] * acc.reshape(nb, tt, d)
    if final:
        y = _rms(y, fg_ref[...])
    y_ref[...] = y


def _outproj(x, o_parts, gate, ada_gate, w, final_g, nb, tt):
    nseq, t, d = x.shape
    grid = (nseq // nb, t // tt)
    xmap = lambda i, j: (i, j, 0)
    final = final_g is not None
    in_specs = [pl.BlockSpec((nb, tt, d), xmap)]
    in_specs += [pl.BlockSpec((nb, tt, o.shape[-1]), xmap) for o in o_parts]
    in_specs += [
        pl.BlockSpec((nb, tt, gate.shape[-1]), xmap),
        pl.BlockSpec((nb, 1, d), lambda i, j: (i, 0, 0)),
        pl.BlockSpec(w.shape, lambda i, j: (0, 0)),
    ]
    args = [x, *o_parts, gate, ada_gate, w]
    if final:
        in_specs.append(pl.BlockSpec((1, 1, d), lambda i, j: (0, 0, 0)))
        args.append(final_g)
    return pl.pallas_call(
        functools.partial(_outproj_kernel, n_parts=len(o_parts), final=final),
        out_shape=jax.ShapeDtypeStruct((nseq, t, d), F32),
        grid=grid,
        in_specs=in_specs,
        out_specs=pl.BlockSpec((nb, tt, d), xmap),
        compiler_params=_params("arbitrary", "arbitrary"),
        name="gate_outproj",
    )(*args)


def _gla_rows(q, k, v, glow, wg, bg, gnorm, states, seq_len):
    r = q.shape[0]
    ns = r // seq_len
    la = jax.nn.log_sigmoid(_dot(glow.astype(BF16), wg) + bg) / GLA_TAU
    ri = lax.broadcasted_iota(jnp.int32, (r, r), 0)
    ci = lax.broadcasted_iota(jnp.int32, (r, r), 1)
    same = (ri // seq_len) == (ci // seq_len)
    causal = same & (ci <= ri)
    cum = jnp.dot(causal.astype(F32), la, preferred_element_type=F32, precision=HIGHEST)
    tot = jnp.dot(same.astype(F32), la, preferred_element_type=F32, precision=HIGHEST)
    q_all = q * (DK_A ** -0.5) * jnp.exp(cum)
    k_all = k * jnp.exp(-cum)
    ke_all = k * jnp.exp(tot - cum)
    dec = jnp.exp(tot)
    er = lax.broadcasted_iota(jnp.int32, (DK_A, DK_A), 0)
    ec = lax.broadcasted_iota(jnp.int32, (DK_A, DK_A), 1)
    eye = (er == ec).astype(F32)
    outs = []
    new_states = [[None] * H_A for _ in range(ns)]
    for h in range(H_A):
        ks = slice(h * DK_A, (h + 1) * DK_A)
        vs = slice(h * DV_A, (h + 1) * DV_A)
        q_in = q_all[:, ks]
        k_end = ke_all[:, ks]
        v_h = v[:, vs]
        att = jnp.where(causal, _dot_nt(q_in.astype(BF16), k_all[:, ks].astype(BF16)), 0.0)
        o = _dot(att.astype(BF16), v_h.astype(BF16))
        o_rows = []
        for j in range(ns):
            rs = slice(j * seq_len, (j + 1) * seq_len)
            s_old = states[j][h]
            o_rows.append(o[rs] + _dot(q_in[rs].astype(BF16), s_old.astype(BF16)))
            d_col = jnp.sum(eye * dec[j * seq_len:j * seq_len + 1, ks], axis=1, keepdims=True)
            new_states[j][h] = s_old * d_col + _dot_tn(k_end[rs].astype(BF16), v_h[rs].astype(BF16))
        o = o_rows[0] if ns == 1 else jnp.concatenate(o_rows, axis=0)
        outs.append(_rms(o, gnorm))
    return outs, new_states


def _gla_prompt_kernel(qkv_ref, glow_ref, wg_ref, bg_ref, gn_ref, o_ref, s_ref, st_sc):
    t = pl.program_id(1)
    tt = qkv_ref.shape[1]
    ch = GLA_CHUNK
    nc = tt // ch
    nq = H_A * DK_A

    @pl.when(t == 0)
    def _():
        st_sc[...] = jnp.zeros_like(st_sc)

    q = qkv_ref[0, :, 0:nq]
    k = qkv_ref[0, :, nq:2 * nq]
    la = jax.nn.log_sigmoid(_dot(glow_ref[0].astype(BF16), wg_ref[...]) + bg_ref[...]) / GLA_TAU
    la_w = jnp.concatenate([la[c * ch:(c + 1) * ch] for c in range(nc)], axis=1)
    ri = lax.broadcasted_iota(jnp.int32, (ch, ch), 0)
    ci = lax.broadcasted_iota(jnp.int32, (ch, ch), 1)
    tril = (ci <= ri).astype(BF16)
    cum_w = _dot(jnp.concatenate([tril] * 3, axis=1), jnp.concatenate(_split_bf16(la_w, 3), axis=0))
    tot_w = jnp.broadcast_to(cum_w[ch - 1:ch], cum_w.shape)
    tall = lambda a: jnp.concatenate([a[:, c * nq:(c + 1) * nq] for c in range(nc)], axis=0)
    cum = tall(cum_w)
    tot = tall(tot_w)
    q_all = q * (DK_A ** -0.5) * jnp.exp(cum)
    k_all = k * jnp.exp(-cum)
    ke_all = k * jnp.exp(tot - cum)
    dec = jnp.exp(tot)
    sub = min(GLA_ATT_ROWS, tt)
    rr = lax.broadcasted_iota(jnp.int32, (sub, sub), 0)
    cc = lax.broadcasted_iota(jnp.int32, (sub, sub), 1)
    causal = (rr // ch == cc // ch) & (cc <= rr)
    eye = (ri[:DK_A, :DK_A] == ci[:DK_A, :DK_A]).astype(F32)
    gn = gn_ref[...]
    for h in range(H_A):
        ks = slice(h * DK_A, (h + 1) * DK_A)
        q_in = q_all[:, ks].astype(BF16)
        k_end = ke_all[:, ks].astype(BF16)
        v_h = qkv_ref[0, :, 2 * nq + h * DV_A:2 * nq + (h + 1) * DV_A].astype(BF16)
        k_in = k_all[:, ks].astype(BF16)
        o_parts = []
        for s0 in range(0, tt, sub):
            ss = slice(s0, s0 + sub)
            att = jnp.where(causal, _dot_nt(q_in[ss], k_in[ss]), 0.0)
            o_parts.append(_dot(att.astype(BF16), v_h[ss]))
        o = jnp.concatenate(o_parts, axis=0)
        s_h = st_sc[h]
        rows_out = []
        for c in range(nc):
            rs = slice(c * ch, (c + 1) * ch)
            rows_out.append(o[rs] + _dot(q_in[rs], s_h.astype(BF16)))
            d_col = jnp.sum(eye * dec[c * ch:c * ch + 1, ks], axis=1, keepdims=True)
            s_h = s_h * d_col + _dot_tn(k_end[rs], v_h[rs])
        st_sc[h] = s_h
        o_ref[0, :, h * DV_A:(h + 1) * DV_A] = _rms(jnp.concatenate(rows_out, axis=0), gn).astype(o_ref.dtype)

    @pl.when(t == pl.num_programs(1) - 1)
    def _():
        s_ref[0] = st_sc[...]


def _gla_prompt(qkv, glow, wg, bg, gn, tt):
    b, t, _ = qkv.shape
    assert t % tt == 0 and tt % GLA_CHUNK == 0
    return pl.pallas_call(
        _gla_prompt_kernel,
        out_shape=[jax.ShapeDtypeStruct((b, t, W_A), BF16), jax.ShapeDtypeStruct((b, H_A, DK_A, DV_A), F32)],
        grid=(b, t // tt),
        in_specs=[
            pl.BlockSpec((1, tt, qkv.shape[-1]), lambda i, j: (i, j, 0)),
            pl.BlockSpec((1, tt, LANE), lambda i, j: (i, j, 0)),
            pl.BlockSpec(wg.shape, lambda i, j: (0, 0)),
            pl.BlockSpec(bg.shape, lambda i, j: (0, 0)),
            pl.BlockSpec(gn.shape, lambda i, j: (0, 0)),
        ],
        out_specs=[
            pl.BlockSpec((1, tt, W_A), lambda i, j: (i, j, 0)),
            pl.BlockSpec((1, H_A, DK_A, DV_A), lambda i, j: (i, 0, 0, 0)),
        ],
        scratch_shapes=[pltpu.VMEM((H_A, DK_A, DV_A), F32)],
        compiler_params=_params("arbitrary", "arbitrary"),
        name="gla_prompt",
    )(qkv, glow, wg, bg, gn)


def _gla_sample_kernel(qkv_ref, glow_ref, s0_ref, wg_ref, bg_ref, gn_ref, o_ref, s_ref):
    nb, ts, _ = qkv_ref.shape
    nq = H_A * DK_A
    r = nb * ts
    qkv = qkv_ref[...].reshape(r, qkv_ref.shape[-1])
    glow = glow_ref[...].reshape(r, LANE)
    states = [[s0_ref[j, h] for h in range(H_A)] for j in range(nb)]
    outs, new_states = _gla_rows(qkv[:, 0:nq], qkv[:, nq:2 * nq], qkv[:, 2 * nq:2 * nq + W_A], glow,
                                 wg_ref[...], bg_ref[...], gn_ref[...], states, ts)
    for h in range(H_A):
        o_ref[:, :, h * DV_A:(h + 1) * DV_A] = outs[h].reshape(nb, ts, DV_A)
        for j in range(nb):
            s_ref[j, h] = new_states[j][h]


def _gla_sample(qkv, glow, s0, wg, bg, gn, nb):
    b, ts, _ = qkv.shape
    assert b % nb == 0 and ts % 8 == 0 and ts <= GLA_CHUNK
    return pl.pallas_call(
        _gla_sample_kernel,
        out_shape=[jax.ShapeDtypeStruct((b, ts, W_A), F32), jax.ShapeDtypeStruct((b, H_A, DK_A, DV_A), F32)],
        grid=(b // nb,),
        in_specs=[
            pl.BlockSpec((nb, ts, qkv.shape[-1]), lambda i: (i, 0, 0)),
            pl.BlockSpec((nb, ts, LANE), lambda i: (i, 0, 0)),
            pl.BlockSpec((nb, H_A, DK_A, DV_A), lambda i: (i, 0, 0, 0)),
            pl.BlockSpec(wg.shape, lambda i: (0, 0)),
            pl.BlockSpec(bg.shape, lambda i: (0, 0)),
            pl.BlockSpec(gn.shape, lambda i: (0, 0)),
        ],
        out_specs=[
            pl.BlockSpec((nb, ts, W_A), lambda i: (i, 0, 0)),
            pl.BlockSpec((nb, H_A, DK_A, DV_A), lambda i: (i, 0, 0, 0)),
        ],
        compiler_params=_params("arbitrary"),
        name="gla_sample",
    )(qkv, glow, s0, wg, bg, gn)


def _attend_cols(kaug_sc, vt_sc, q_aug, i, tq, acc_sc):
    ncol = q_aug.shape[0]
    t = kaug_sc.shape[0]
    gk = min(max(KV_GROUP_KEYS, tq), t)
    assert gk % tq == 0 and t % gk == 0
    kr = lax.broadcasted_iota(jnp.int32, (gk, ncol), 0)
    qc = lax.broadcasted_iota(jnp.int32, (gk, ncol), 1) % tq
    for v in range(t // gk):
        head = v * gk

        @pl.when((i * tq) // gk == v)
        def _(head=head):
            s_t = _dot_nt(kaug_sc[head:head + gk, :], q_aug)
            s_t = jnp.where(kr + head <= qc + i * tq, s_t, NEG)
            m = jnp.max(s_t, axis=0, keepdims=True)
            if head > 0:
                s_h = _dot_nt(kaug_sc[0:head, :], q_aug)
                m = jnp.maximum(m, jnp.max(s_h, axis=0, keepdims=True))
            acc = _dot(vt_sc[:, head:head + gk], jnp.exp2(s_t - m).astype(BF16))
            if head > 0:
                acc = acc + _dot(vt_sc[:, 0:head], jnp.exp2(s_h - m).astype(BF16))
            acc_sc[...] = acc


def _log2e_coef(idx, base):
    r = idx - base
    out = jnp.zeros(idx.shape, F32)
    for n, piece in enumerate(LOG2E_PIECES):
        out = jnp.where((r == n) | (r == n + 3), piece, out)
    return out


def _fill_kv(k_ref, v_ref, kaug_sc, vt_sc, tk, aug_fn, km_sc=None):
    nblk = kaug_sc.shape[0] // tk
    dv = v_ref.shape[-1]
    rowv = lax.broadcasted_iota(jnp.int32, (VT_ROWS - dv, tk), 0)
    ones_rows = jnp.where(rowv == 0, 1.0, 0.0).astype(BF16)
    for n in range(nblk):
        rows = slice(n * tk, (n + 1) * tk)
        kt = k_ref[0, rows, :]
        if km_sc is not None:
            km_sc[n:n + 1, :] = jnp.mean(kt, axis=0, keepdims=True)
        kaug_sc[rows, 0:LANE] = kt.astype(BF16)
        kaug_sc[rows, LANE:] = aug_fn(n).astype(BF16)
        vt_sc[0:dv, rows] = v_ref[0, rows, :].T.astype(BF16)
        vt_sc[dv:, rows] = ones_rows


def _lambda(l_ref, lam_init):
    lv = l_ref[...]
    s1 = jnp.sum(lv[0:1] * lv[1:2], axis=1, keepdims=True)
    s2 = jnp.sum(lv[2:3] * lv[3:4], axis=1, keepdims=True)
    return jnp.exp(s1) - jnp.exp(s2) + lam_init


def _diff_prompt_kernel(slope_ref, q_ref, k_ref, v_ref, l_ref, gn_ref, o_ref, kaug_sc, vt_sc, acc_sc, *, lam_init):
    h = pl.program_id(1)
    i = pl.program_id(2)
    tq = q_ref.shape[1]

    @pl.when(i == 0)
    def _():
        slope = slope_ref[h]
        ft = min(FILL_TILE, tq)
        c = lax.broadcasted_iota(jnp.int32, (ft, LANE), 0).astype(F32)
        col = lax.broadcasted_iota(jnp.int32, (ft, LANE), 1)

        def aug(n):
            return jnp.where(col < 3, slope * float(ft * n), jnp.where(col < 6, slope * c, 0.0))

        _fill_kv(k_ref, v_ref, kaug_sc, vt_sc, ft, aug)

    q = q_ref[0]
    lane = lax.broadcasted_iota(jnp.int32, q.shape, 1)
    zero = jnp.zeros_like(q)
    lcoef = _log2e_coef(lane, 0).astype(BF16)
    q_aug = jnp.concatenate([
        jnp.concatenate([jnp.where(lane < DK_B, q, zero), lcoef], axis=1),
        jnp.concatenate([jnp.where(lane >= DK_B, q, zero), lcoef], axis=1)], axis=0)
    _attend_cols(kaug_sc, vt_sc, q_aug, i, tq, acc_sc)
    acc = acc_sc[...]
    o_t = acc[0:DV_B] / acc[DV_B:DV_B + 1]
    lam = _lambda(l_ref, lam_init)
    o = (o_t[:, :tq] - lam * o_t[:, tq:]).T
    o_ref[0] = (_rms(o, gn_ref[...]) * (1.0 - lam_init)).astype(o_ref.dtype)


def _diff_prompt(q, k, v, slopes, lvec, gn, lam_init, tq):
    b, t, _ = q.shape
    assert t % tq == 0
    nblk = t // tq
    kv_spec = pl.BlockSpec((1, t, LANE), lambda bi, h, i, s: (bi, 0, h))
    return pl.pallas_call(
        functools.partial(_diff_prompt_kernel, lam_init=lam_init),
        out_shape=jax.ShapeDtypeStruct((b, t, W_B), BF16),
        grid_spec=pltpu.PrefetchScalarGridSpec(
            num_scalar_prefetch=1,
            grid=(b, H_B, nblk),
            in_specs=[
                pl.BlockSpec((1, tq, LANE), lambda bi, h, i, s: (bi, i, h)),
                kv_spec,
                kv_spec,
                pl.BlockSpec(lvec.shape, lambda bi, h, i, s: (0, 0)),
                pl.BlockSpec(gn.shape, lambda bi, h, i, s: (0, 0)),
            ],
            out_specs=pl.BlockSpec((1, tq, LANE), lambda bi, h, i, s: (bi, i, h)),
            scratch_shapes=[
                pltpu.VMEM((t, 2 * LANE), BF16),
                pltpu.VMEM((VT_ROWS, t), BF16),
                pltpu.VMEM((VT_ROWS, 2 * tq), F32),
            ],
        ),
        compiler_params=_params("arbitrary", "arbitrary", "arbitrary"),
        name="diff_prompt",
    )(slopes, q, k, v, lvec, gn)


def _topk_mask(g, valid, n_axis):
    nb = g.shape[n_axis]
    g = jnp.where(valid, g, -jnp.inf)
    idx = lax.broadcasted_iota(jnp.int32, g.shape, n_axis)
    rank = jnp.zeros(g.shape, jnp.int32)
    for m in range(nb):
        gm = lax.slice_in_dim(g, m, m + 1, axis=n_axis)
        beats = (gm > g) | ((gm == g) & (m < idx))
        rank = rank + beats.astype(jnp.int32)
    return (rank < MOBA_TOPK) & valid


def _moba_prompt_kernel(slope_ref, q_ref, k_ref, v_ref, o_ref, km_sc, kaug_sc, vt_sc, acc_sc):
    h = pl.program_id(1)
    i = pl.program_id(2)
    tq = q_ref.shape[1]
    bs = MOBA_BLOCK
    nblk = km_sc.shape[0]

    @pl.when(i == 0)
    def _():
        slope = slope_ref[h]
        c = lax.broadcasted_iota(jnp.int32, (bs, LANE), 0).astype(F32)
        col = lax.broadcasted_iota(jnp.int32, (bs, LANE), 1)

        def aug(n):
            return jnp.where(col == n, 1.0, jnp.where(col < nblk, 0.0, jnp.where(
                col < nblk + 3, slope * float(bs * n), jnp.where(col < nblk + 6, slope * c, 0.0))))

        _fill_kv(k_ref, v_ref, kaug_sc, vt_sc, bs, aug, km_sc)

    q = q_ref[0]
    g = _dot_nt_split(km_sc[...], q)
    blk = lax.broadcasted_iota(jnp.int32, g.shape, 0)
    own = i * (tq // bs) + lax.broadcasted_iota(jnp.int32, g.shape, 1) // bs
    sel = _topk_mask(g, blk < own, 0)
    selb = jnp.where(sel | (blk == own), 0.0, NEG)
    row = lax.broadcasted_iota(jnp.int32, (LANE, tq), 0)
    coef = jnp.concatenate([selb, jnp.zeros((LANE - nblk, tq), F32)], axis=0)
    coef = coef + _log2e_coef(row, nblk)
    q_aug = jnp.concatenate([q * (DH_C ** -0.5 * LOG2E), coef.T], axis=1).astype(BF16)
    _attend_cols(kaug_sc, vt_sc, q_aug, i, tq, acc_sc)
    acc = acc_sc[...]
    o_ref[0] = (acc[0:DH_C] / acc[DH_C:DH_C + 1]).T.astype(o_ref.dtype)


def _moba_prompt(q, k, v, slopes):
    b, t, _ = q.shape
    bs = MOBA_BLOCK
    tq = min(MOBA_Q_TILE, t)
    assert t % tq == 0 and tq % bs == 0
    nblk = t // bs
    assert nblk + 6 <= LANE
    kv_spec = pl.BlockSpec((1, t, LANE), lambda bi, h, i, s: (bi, 0, h))
    q_spec = pl.BlockSpec((1, tq, LANE), lambda bi, h, i, s: (bi, i, h))
    return pl.pallas_call(
        _moba_prompt_kernel,
        out_shape=jax.ShapeDtypeStruct((b, t, W_C), BF16),
        grid_spec=pltpu.PrefetchScalarGridSpec(
            num_scalar_prefetch=1,
            grid=(b, H_C, t // tq),
            in_specs=[q_spec, kv_spec, kv_spec],
            out_specs=q_spec,
            scratch_shapes=[
                pltpu.VMEM((nblk, DH_C), F32),
                pltpu.VMEM((t, 2 * LANE), BF16),
                pltpu.VMEM((VT_ROWS, t), BF16),
                pltpu.VMEM((VT_ROWS, tq), F32),
            ],
        ),
        compiler_params=_params("arbitrary", "arbitrary", "arbitrary"),
        name="moba_prompt",
    )(slopes, q, k, v)


def _score_cols_mask(n_heads, n_maps, dk, ts):
    c = np.arange(LANE)[:, None]
    f = np.arange(n_heads * n_maps * dk)[None, :]
    ms = LANE // n_maps
    return (((c % ms) // ts == f // (n_maps * dk)) & (c // ms == (f % (n_maps * dk)) // dk)).astype(np.float32)


def _score_bias(n_heads, n_maps, ts, past_len, lpad):
    j = np.arange(lpad)[:, None]
    c = np.arange(LANE)[None, :]
    hh = (c % (LANE // n_maps)) // ts
    rel = past_len + c % ts - j
    slope = np.array([2.0 ** (-8.0 * (h + 1) / n_heads) for h in range(LANE // ts + 1)], np.float32)[hh]
    bias = np.where(hh < n_heads, -slope * rel.astype(np.float32), np.float32(0.0))
    return np.where((rel >= 0) & (j < past_len + ts), bias, np.float32(NEG)).astype(np.float32)


def _head_pair(p_ref, j, n_heads, page):
    rows = p_ref.at[0]
    return jnp.concatenate([rows[pl.ds(2 * j, page, stride=n_heads), :],
                            rows[pl.ds(2 * j + 1, page, stride=n_heads), :]], axis=1).astype(BF16)


def _page_scores(q, wmask_ref, kp_refs, kn_ref, s_sc, past_len, n_heads):
    ts, width = q.shape
    ncol = s_sc.shape[1]
    page = kp_refs[0].shape[1] // n_heads
    wt = (jnp.broadcast_to(q[None], (ncol // ts, ts, width)).reshape(ncol, width) * wmask_ref[...]).astype(BF16)
    for g, kp_ref in enumerate(kp_refs):
        acc = None
        for j in range(n_heads // 2):
            part = _dot_nt(_head_pair(kp_ref, j, n_heads, page), wt[:, 2 * j * LANE:(2 * j + 2) * LANE])
            acc = part if acc is None else acc + part
        s_sc[g * page:(g + 1) * page, :] = acc
    s_sc[past_len:past_len + ts, :] = _dot_nt(kn_ref[0].astype(BF16), wt)
    s_sc[past_len + ts:, :] = jnp.zeros((s_sc.shape[0] - past_len - ts, ncol), F32)


def _page_values(a_t, vp_refs, vn_ref, past_len, n_heads):
    page = vp_refs[0].shape[1] // n_heads
    ts, width = vn_ref.shape[1], vn_ref.shape[2]
    acc = [jnp.zeros((2 * ts, 2 * LANE), F32) for _ in range(n_heads // 2)]
    for g, vp_ref in enumerate(vp_refs):
        at = a_t[g * page:(g + 1) * page].T
        for j in range(n_heads // 2):
            acc[j] = acc[j] + _dot(at[2 * ts * j:2 * ts * (j + 1)].astype(BF16), _head_pair(vp_ref, j, n_heads, page))
    v_tail = jnp.concatenate([vn_ref[0], jnp.zeros((LANE - ts, width), F32)], axis=0)
    o_tail = _dot(a_t[past_len:].T.astype(BF16), v_tail.astype(BF16))
    outs = []
    for h in range(n_heads):
        j, r = divmod(h, 2)
        outs.append(acc[j][r * ts:(r + 1) * ts, r * LANE:(r + 1) * LANE]
                    + o_tail[h * ts:(h + 1) * ts, h * LANE:(h + 1) * LANE])
    return outs


def _softmax_rows(s):
    e = jnp.exp(s - jnp.max(s, axis=0, keepdims=True))
    return e * (1.0 / jnp.sum(e, axis=0, keepdims=True))


def _diff_sample_kernel(pt_ref, q_ref, kn_ref, vn_ref, wmask_ref, bias_ref, l_ref, gn_ref, *rest,
                        n_pages, lam_init, past_len):
    kp_refs, vp_refs = rest[:n_pages], rest[n_pages:2 * n_pages]
    o_ref, s_sc = rest[2 * n_pages:]
    ts = q_ref.shape[1]
    _page_scores(q_ref[0], wmask_ref, kp_refs, kn_ref, s_sc, past_len, H_B)
    pr = _softmax_rows(s_sc[...] + bias_ref[...])
    lam = _lambda(l_ref, lam_init)
    a_t = pr - lam * pltpu.roll(pr, LANE // 2, axis=1)
    gn = gn_ref[...]
    for h, o in enumerate(_page_values(a_t, vp_refs, vn_ref, past_len, H_B)):
        o_ref[0, :, h * DV_B:(h + 1) * DV_B] = _rms(o, gn) * (1.0 - lam_init)


def _moba_sample_kernel(pt_ref, q_ref, kn_ref, vn_ref, wmask_ref, bias_ref, *rest, n_pages, past_len):
    kp_refs, vp_refs = rest[:n_pages], rest[n_pages:2 * n_pages]
    o_ref, s_sc = rest[2 * n_pages:]
    ts = q_ref.shape[1]
    bs = MOBA_BLOCK
    nbp = past_len // bs
    _page_scores(q_ref[0] * (DH_C ** -0.5), wmask_ref, kp_refs, kn_ref, s_sc, past_len, H_C)
    s = s_sc[...]
    blocks = [s[n * bs:(n + 1) * bs] for n in range(nbp)]
    g = jnp.concatenate([jnp.sum(b, axis=0, keepdims=True) for b in blocks], axis=0)
    selb = jnp.where(_topk_mask(g, jnp.full(g.shape, True), 0), 0.0, NEG)
    bias = bias_ref[...]
    parts = [blocks[n] + bias[n * bs:(n + 1) * bs] + selb[n:n + 1] for n in range(nbp)]
    parts.append(s[past_len:] + bias[past_len:])
    pr = _softmax_rows(jnp.concatenate(parts, axis=0))
    for h, o in enumerate(_page_values(pr, vp_refs, vn_ref, past_len, H_C)):
        o_ref[0, :, h * DH_C:(h + 1) * DH_C] = o


def _paged_call(body, name, q, k_new, v_new, cache_k, cache_v, page_table, page_base, n_heads, n_maps, extra):
    b, ts, width = q.shape
    n_pages = page_table.shape[1]
    page = cache_k.shape[1] // n_heads
    vwidth = v_new.shape[2]
    past_len = n_pages * page
    assert n_maps * n_heads * ts <= LANE and ts % 8 == 0 and n_heads % 2 == 0
    assert width == n_heads * LANE and vwidth == n_heads * LANE and cache_k.shape[2] == LANE
    ncol = LANE
    lpad = past_len + LANE
    wmask = jnp.asarray(_score_cols_mask(n_heads, n_maps, width // (n_heads * n_maps), ts))
    bias = jnp.asarray(_score_bias(n_heads, n_maps, ts, past_len, lpad))
    const = lambda a: pl.BlockSpec(a.shape, lambda bi, pt: (0,) * a.ndim)
    row = lambda w: pl.BlockSpec((1, ts, w), lambda bi, pt: (bi, 0, 0))
    page_spec = lambda g: pl.BlockSpec((1, page * n_heads, LANE),
                                       lambda bi, pt: (page_base + pt[bi * n_pages + g], 0, 0))
    in_specs = [row(width), row(width), row(vwidth), const(wmask), const(bias)] + [const(a) for a in extra]
    in_specs += [page_spec(g) for g in range(n_pages)] * 2
    return pl.pallas_call(
        functools.partial(body, n_pages=n_pages, past_len=past_len),
        out_shape=jax.ShapeDtypeStruct((b, ts, vwidth), F32),
        grid_spec=pltpu.PrefetchScalarGridSpec(
            num_scalar_prefetch=1,
            grid=(b,),
            in_specs=in_specs,
            out_specs=row(vwidth),
            scratch_shapes=[pltpu.VMEM((lpad, ncol), F32)],
        ),
        compiler_params=_params("arbitrary"),
        name=name,
    )(page_table.reshape(-1), q, k_new, v_new, wmask, bias, *extra, *([cache_k] * n_pages), *([cache_v] * n_pages))


def _diff_sample(q, k_new, v_new, cache_k, cache_v, page_table, page_base, lvec, gn, lam_init):
    body = functools.partial(_diff_sample_kernel, lam_init=lam_init)
    return _paged_call(body, "diff_sample", q, k_new, v_new, cache_k, cache_v, page_table, page_base, H_B, 2,
                       [lvec, gn])


def _moba_sample(q, k_new, v_new, cache_k, cache_v, page_table, page_base):
    past_len = page_table.shape[1] * cache_k.shape[1] // H_C
    assert past_len % MOBA_BLOCK == 0 and past_len // MOBA_BLOCK >= MOBA_TOPK and q.shape[1] <= MOBA_BLOCK
    return _paged_call(_moba_sample_kernel, "moba_sample", q, k_new, v_new, cache_k, cache_v, page_table, page_base,
                       H_C, 1, [])


def _alibi_slopes(n):
    return jnp.array([2.0 ** (-8.0 * (h + 1) / n) for h in range(n)], dtype=F32)


def kernel(x_prompt, x_sample, c_prompt, c_sample, state_gla, cache_diff_k, cache_diff_v, cache_moba_k, cache_moba_v,
           page_table, norm_g, w_ada, b_ada, w_in_e, w_gla_gate, b_gla_gate, g_gla_norm, lam_q1, lam_k1, lam_q2,
           lam_k2, g_diff_norm, w_out_e, w_in_o, w_out_o, final_g):
    bp, tp, d = x_prompt.shape
    bs_, ts, _ = x_sample.shape
    depth = norm_g.shape[0]
    n_pool, page = cache_diff_k.shape[1], cache_diff_k.shape[2]
    assert d == D_MODEL

    tt_p = min(512, tp)
    nb_s = min(64, bs_)
    tq_diff = min(512, tp)

    mrows = bp + bs_
    mpad = -(-mrows // 8) * 8
    c_all = jnp.concatenate([c_prompt, c_sample, jnp.zeros((mpad - mrows, d), F32)], axis=0)
    mod = _ada(c_all, w_ada, b_ada)

    def mod_parts(l, lo, hi):
        m = mod[l, lo:hi]
        return m[:, None, 0:d], m[:, None, d:2 * d], m[:, None, 2 * d:3 * d]

    slopes_b = _alibi_slopes(H_B)
    slopes_c = _alibi_slopes(H_C)
    nqa = H_A * DK_A
    yp, ys = x_prompt, x_sample
    gla_p, gla_s, dk_p, dv_p, dk_s, dv_s, mk_p, mv_p, mk_s, mv_s = ([] for _ in range(10))
    for l in range(depth):
        shift_p, scale_p, gate_p = mod_parts(l, 0, bp)
        shift_s, scale_s, gate_s = mod_parts(l, bp, bp + bs_)
        g_l = norm_g[l].reshape(1, 1, d)
        last = l == depth - 1
        fg = final_g.reshape(1, 1, d) if last else None
        if l % 2 == 0:
            e = l // 2
            w = w_in_e[e]
            c0 = 2 * nqa + W_A
            c1 = c0 + GLA_LOWRANK
            w_cat = jnp.concatenate([w[:, :c0], w[:, c0:c1], jnp.zeros((d, LANE - GLA_LOWRANK), F32), w[:, c1:]],
                                    axis=1).astype(BF16)
            o0 = c0 + LANE
            qmul = DK_B ** -0.5
            splits_s = [(0, c0, 1.0, 0), (c0, LANE, 1.0, 0), (o0, 2 * H_B * DK_B, qmul, 0),
                        (o0 + 512, 512, 1.0, 0), (o0 + 1024, 512, 1.0, 0), (o0 + 1536, W_A + W_B, 1.0, 0)]
            splits = [sp[:3] + (nh,) for sp, nh in zip(splits_s, (0, 0, 0, H_B, H_B, 0))]
            splits[2] = splits[2][:2] + (qmul * LOG2E, 0)
            dtypes = [F32, F32, BF16, F32, F32, BF16]
            wg = jnp.concatenate([w_gla_gate[e], jnp.zeros((LANE - GLA_LOWRANK, nqa), F32)], axis=0).astype(BF16)
            bg = b_gla_gate[e].reshape(1, nqa)
            gn_a = g_gla_norm[e].reshape(1, DV_A)
            gn_b = g_diff_norm[e].reshape(1, DV_B)
            lvec = jnp.stack([lam_q1[e], lam_k1[e], lam_q2[e], lam_k2[e]])
            lam_init = 0.8 - 0.6 * math.exp(-0.3 * l)
            w_out = w_out_e[e].astype(BF16)

            qkv_a, glow, q_b, k_b, v_b, gate, k_hd, v_hd = _inproj(yp, shift_p, scale_p, g_l, w_cat, splits, dtypes,
                                                                   1, tt_p)
            o_a, s_new = _gla_prompt(qkv_a, glow, wg, bg, gn_a, tt_p)
            o_b = _diff_prompt(q_b, k_b, v_b, slopes_b, lvec, gn_b, lam_init, tq_diff)
            yp = _outproj(yp, [o_a, o_b], gate, gate_p, w_out, fg, 1, tt_p)
            gla_p.append(s_new)
            dk_p.append(k_hd.reshape(bp, tp, H_B, 2 * DK_B))
            dv_p.append(v_hd.reshape(bp, tp, H_B, DV_B))

            dtypes_s = [F32] * len(dtypes)
            qkv_a, glow, q_b, k_b, v_b, gate = _inproj(ys, shift_s, scale_s, g_l, w_cat, splits_s, dtypes_s, nb_s, ts)
            o_a, s_new = _gla_sample(qkv_a, glow, state_gla[e], wg, bg, gn_a, min(8, bs_))
            ck = cache_diff_k.reshape(-1, page * H_B, 2 * DK_B)
            cv = cache_diff_v.reshape(-1, page * H_B, DV_B)
            o_b = _diff_sample(q_b, k_b, v_b, ck, cv, page_table, e * n_pool, lvec, gn_b, lam_init)
            ys = _outproj(ys, [o_a, o_b], gate, gate_s, w_out, fg, nb_s, ts)
            gla_s.append(s_new)
            dk_s.append(k_b.reshape(bs_, ts, H_B, 2 * DK_B))
            dv_s.append(v_b.reshape(bs_, ts, H_B, DV_B))
        else:
            o = l // 2
            w_cat = w_in_o[o].astype(BF16)
            splits = [(0, W_C, 1.0, 0), (W_C, W_C, 1.0, H_C), (2 * W_C, W_C, 1.0, H_C), (3 * W_C, W_C, 1.0, 0)]
            splits_s = [sp[:3] + (0,) for sp in splits]
            dtypes = [F32, F32, F32, BF16]
            dtypes_s = [F32] * len(dtypes)
            w_out = w_out_o[o].astype(BF16)

            q, k, v, gate, k_hd, v_hd = _inproj(yp, shift_p, scale_p, g_l, w_cat, splits, dtypes, 1, tt_p)
            att = _moba_prompt(q, k, v, slopes_c)
            yp = _outproj(yp, [att], gate, gate_p, w_out, fg, 1, tt_p)
            mk_p.append(k_hd.reshape(bp, tp, H_C, DH_C))
            mv_p.append(v_hd.reshape(bp, tp, H_C, DH_C))

            q, k, v, gate = _inproj(ys, shift_s, scale_s, g_l, w_cat, splits_s, dtypes_s, nb_s, ts)
            ck = cache_moba_k.reshape(-1, page * H_C, DH_C)
            cv = cache_moba_v.reshape(-1, page * H_C, DH_C)
            att = _moba_sample(q, k, v, ck, cv, page_table, o * n_pool)
            ys = _outproj(ys, [att], gate, gate_s, w_out, fg, nb_s, ts)
            mk_s.append(k.reshape(bs_, ts, H_C, DH_C))
            mv_s.append(v.reshape(bs_, ts, H_C, DH_C))
    return (yp, ys, jnp.stack(gla_p), jnp.stack(gla_s), jnp.stack(dk_p), jnp.stack(dv_p), jnp.stack(dk_s),
            jnp.stack(dv_s), jnp.stack(mk_p), jnp.stack(mv_p), jnp.stack(mk_s), jnp.stack(mv_s))
```
